```python
import math
import jax, jax.numpy as jnp
from jax import lax
import numpy as np

D_MODEL = 1024
BATCH = 4
SEQ = 8192
DEPTH = 2

N_MIXERS = 2
RET_HEADS = 4
RET_QK_DIM = D_MODEL // RET_HEADS
RET_V_DIM = 2 * D_MODEL // RET_HEADS
RET_CHUNK = 128
RET_IN = 2 * RET_HEADS * RET_QK_DIM + 2 * RET_HEADS * RET_V_DIM
SWA_Q_HEADS = 16
SWA_KV_HEADS = 2
SWA_GROUP = SWA_Q_HEADS // SWA_KV_HEADS
SWA_HEAD_DIM = 64
WINDOW = 128
SWA_BLOCK = 128
SWA_IN = (SWA_Q_HEADS + 2 * SWA_KV_HEADS) * SWA_HEAD_DIM
N_EXPERTS = 32
TOP_K = 4
D_FF = D_MODEL
SWIGLU_ALPHA = 1.702
SWIGLU_LIMIT = 7.0
MOE_BLOCK = 128
LN_EPS = 1e-5
GN_EPS = 1e-6
DEEPNORM_ALPHA = (2 * DEPTH) ** 0.25
DEEPNORM_BETA = (8 * DEPTH) ** -0.25
N_RET_LAYERS = (DEPTH + 1) // 2
N_SWA_LAYERS = DEPTH // 2

kernel_name = "retnet_swa_sink_moe_deepnorm_hybrid"


def layer_norm(x, g, b):
    xf = x.astype(jnp.float32)
    mu = jnp.mean(xf, axis=-1, keepdims=True)
    xc = xf - mu
    var = jnp.mean(xc * xc, axis=-1, keepdims=True)
    return (xc * lax.rsqrt(var + LN_EPS) * g.astype(jnp.float32) + b.astype(jnp.float32)).astype(x.dtype)


def retention(x, w_in, w_out):
    B, S, _ = x.shape
    H, dk, dv, C = RET_HEADS, RET_QK_DIM, RET_V_DIM, RET_CHUNK
    nc = S // C
    proj = x @ w_in
    q, k, v, g = jnp.split(proj, [H * dk, 2 * H * dk, 2 * H * dk + H * dv], axis=-1)

    def chunks(t, d):
        return t.astype(jnp.float32).reshape(B, nc, C, H, d).transpose(1, 0, 3, 2, 4)

    qc = chunks(q, dk)
    kc = chunks(k, dk) * (dk ** -0.5)
    vc = chunks(v, dv)
    log_g = jnp.log(1.0 - 2.0 ** (-5.0 - jnp.arange(H, dtype=jnp.float32)))
    pos = jnp.arange(C, dtype=jnp.float32)
    rel = pos[:, None] - pos[None, :]
    decay_mask = jnp.where(rel >= 0, jnp.exp(log_g[:, None, None] * jnp.maximum(rel, 0.0)), 0.0)
    q_decay = jnp.exp(log_g[:, None] * (pos + 1.0))[..., None]
    k_decay = jnp.exp(log_g[:, None] * (C - 1.0 - pos))[..., None]
    chunk_decay = jnp.exp(log_g * C)[:, None, None]

    def step(state, inp):
        qi, ki, vi = inp
        scores = jnp.einsum('bhik,bhjk->bhij', qi, ki) * decay_mask
        o = jnp.einsum('bhij,bhjv->bhiv', scores, vi) + jnp.einsum('bhik,bhkv->bhiv', qi * q_decay, state)
        state = state * chunk_decay + jnp.einsum('bhjk,bhjv->bhkv', ki * k_decay, vi)
        return state, o

    state0 = jnp.zeros((B, H, dk, dv), jnp.float32)
    _, o = lax.scan(step, state0, (qc, kc, vc))
    o = o.transpose(1, 0, 3, 2, 4).reshape(B, S, H, dv)
    mu = jnp.mean(o, axis=-1, keepdims=True)
    oc = o - mu
    var = jnp.mean(oc * oc, axis=-1, keepdims=True)
    o = (oc * lax.rsqrt(var + GN_EPS)).reshape(B, S, H * dv).astype(x.dtype)
    return (jax.nn.silu(g) * o) @ w_out


def sliding_window_attention(x, w_in, b_in, w_out, b_out, sinks):
    B, S, _ = x.shape
    Hq, Hkv, G, hd, C = SWA_Q_HEADS, SWA_KV_HEADS, SWA_GROUP, SWA_HEAD_DIM, SWA_BLOCK
    nb = S // C
    proj = x @ w_in + b_in
    q, k, v = jnp.split(proj, [Hq * hd, Hq * hd + Hkv * hd], axis=-1)
    q = q.reshape(B, nb, C, Hkv, G, hd)
    k = k.reshape(B, nb, C, Hkv, hd)
    v = v.reshape(B, nb, C, Hkv, hd)

    def with_prev(t):
        prev = jnp.concatenate([jnp.zeros_like(t[:, :1]), t[:, :-1]], axis=1)
        return jnp.concatenate([prev, t], axis=2)

    kb, vb = with_prev(k), with_prev(v)
    scores = jnp.einsum('bnqhgd,bnkhd->bnhgqk', q, kb).astype(jnp.float32) * (hd ** -0.5)
    qi = jnp.arange(C)[:, None]
    kj = jnp.arange(2 * C)[None, :]
    dist = qi + C - kj
    blk = jnp.arange(nb)[:, None, None]
    valid = (dist >= 0) & (dist < WINDOW) & (blk * C + kj - C >= 0)
    slopes = (2.0 ** (-8.0 * jnp.arange(1, Hq + 1, dtype=jnp.float32) / Hq)).reshape(Hkv, G)
    scores = scores - slopes[:, :, None, None] * dist.astype(jnp.float32)
    scores = jnp.where(valid[None, :, None, None], scores, -jnp.inf)
    sink = jnp.broadcast_to(sinks.astype(jnp.float32).reshape(Hkv, G, 1, 1), scores.shape[:-1] + (1,))
    probs = jax.nn.softmax(jnp.concatenate([scores, sink], axis=-1), axis=-1)[..., :-1]
    o = jnp.einsum('bnhgqk,bnkhd->bnqhgd', probs.astype(x.dtype), vb).reshape(B, S, Hq * hd)
    return o @ w_out + b_out


def moe(x, w_router, b_router, w1, b1, w2, b2):
    B, S, D = x.shape
    T = B * S
    xt = x.reshape(T, D)
    logits = (xt @ w_router + b_router).astype(jnp.float32)
    top_val, top_idx = lax.top_k(logits, TOP_K)
    gates = jax.nn.softmax(top_val, axis=-1).astype(x.dtype).reshape(-1)
    flat_e = top_idx.reshape(-1)
    flat_tok = jnp.arange(T * TOP_K, dtype=jnp.int32) // TOP_K
    order = jnp.argsort(flat_e)
    sorted_e = flat_e[order]
    counts = jnp.bincount(flat_e, length=N_EXPERTS)
    padded = (counts + MOE_BLOCK - 1) // MOE_BLOCK * MOE_BLOCK
    start_sorted = jnp.cumsum(counts) - counts
    padded_end = jnp.cumsum(padded)
    start_padded = padded_end - padded
    dest = start_padded[sorted_e] + jnp.arange(T * TOP_K) - start_sorted[sorted_e]
    n_blocks = -(-(T * TOP_K) // MOE_BLOCK) + N_EXPERTS
    n_slots = n_blocks * MOE_BLOCK
    slot_tok = jnp.zeros((n_slots,), jnp.int32).at[dest].set(flat_tok[order])
    slot_gate = jnp.zeros((n_slots,), x.dtype).at[dest].set(gates[order])
    block_expert = jnp.minimum(
        jnp.searchsorted(padded_end, jnp.arange(n_blocks) * MOE_BLOCK, side='right'), N_EXPERTS - 1)
    x_blocks = xt[slot_tok].reshape(n_blocks, MOE_BLOCK, D)

    def expert_block(args):
        xb, e = args
        h = xb @ w1[e] + b1[e]
        glu = jnp.minimum(h[:, :D_FF], SWIGLU_LIMIT)
        lin = jnp.clip(h[:, D_FF:], -SWIGLU_LIMIT, SWIGLU_LIMIT)
        act = glu * jax.nn.sigmoid(SWIGLU_ALPHA * glu) * (lin + 1.0)
        return act @ w2[e] + b2[e]

    y = lax.map(expert_block, (x_blocks, block_expert)).reshape(n_slots, D)
    out = jnp.zeros_like(xt).at[slot_tok].add(y * slot_gate[:, None])
    return out.reshape(B, S, D)


def setup_inputs(seed: int = 0) -> dict:
    key = jax.random.key(seed)
    ks = jax.random.split(key, 20)
    f32 = jnp.float32
    beta = DEEPNORM_BETA
    x = jax.random.normal(ks[0], (BATCH, SEQ, D_MODEL), f32)
    ret_scale = jnp.concatenate([
        jnp.ones((2 * RET_HEADS * RET_QK_DIM,), f32),
        jnp.full((RET_HEADS * RET_V_DIM,), beta, f32),
        jnp.ones((RET_HEADS * RET_V_DIM,), f32)])
    w_ret_in = jax.random.normal(ks[1], (N_RET_LAYERS, D_MODEL, RET_IN), f32) * (D_MODEL ** -0.5) * ret_scale
    w_ret_out = jax.random.normal(ks[2], (N_RET_LAYERS, RET_HEADS * RET_V_DIM, D_MODEL), f32) * ((RET_HEADS * RET_V_DIM) ** -0.5) * beta
    swa_scale = jnp.concatenate([
        jnp.ones(((SWA_Q_HEADS + SWA_KV_HEADS) * SWA_HEAD_DIM,), f32),
        jnp.full((SWA_KV_HEADS * SWA_HEAD_DIM,), beta, f32)])
    w_swa_in = jax.random.normal(ks[3], (N_SWA_LAYERS, D_MODEL, SWA_IN), f32) * (D_MODEL ** -0.5) * swa_scale
    b_swa_in = 0.01 * jax.random.normal(ks[4], (N_SWA_LAYERS, SWA_IN), f32)
    w_swa_out = jax.random.normal(ks[5], (N_SWA_LAYERS, SWA_Q_HEADS * SWA_HEAD_DIM, D_MODEL), f32) * ((SWA_Q_HEADS * SWA_HEAD_DIM) ** -0.5) * beta
    b_swa_out = 0.01 * jax.random.normal(ks[6], (N_SWA_LAYERS, D_MODEL), f32)
    swa_sinks = 0.5 * jax.random.normal(ks[7], (N_SWA_LAYERS, SWA_Q_HEADS), f32)
    ln_mix_g = 1.0 + 0.05 * jax.random.normal(ks[8], (DEPTH, D_MODEL), f32)
    ln_mix_b = 0.01 * jax.random.normal(ks[9], (DEPTH, D_MODEL), f32)
    w_router = jax.random.normal(ks[10], (DEPTH, D_MODEL, N_EXPERTS), f32) * (D_MODEL ** -0.5)
    b_router = 0.01 * jax.random.normal(ks[11], (DEPTH, N_EXPERTS), f32)
    w_exp_in = jax.random.normal(ks[12], (DEPTH, N_EXPERTS, D_MODEL, 2 * D_FF), f32) * (D_MODEL ** -0.5)
    b_exp_in = 0.01 * jax.random.normal(ks[13], (DEPTH, N_EXPERTS, 2 * D_FF), f32)
    w_exp_out = jax.random.normal(ks[14], (DEPTH, N_EXPERTS, D_FF, D_MODEL), f32) * (D_FF ** -0.5) * beta
    b_exp_out = 0.01 * jax.random.normal(ks[15], (DEPTH, N_EXPERTS, D_MODEL), f32)
    ln_ffn_g = 1.0 + 0.05 * jax.random.normal(ks[16], (DEPTH, D_MODEL), f32)
    ln_ffn_b = 0.01 * jax.random.normal(ks[17], (DEPTH, D_MODEL), f32)
    return {"x": x, "w_ret_in": w_ret_in, "w_ret_out": w_ret_out,
            "w_swa_in": w_swa_in, "b_swa_in": b_swa_in, "w_swa_out": w_swa_out,
            "b_swa_out": b_swa_out, "swa_sinks": swa_sinks,
            "ln_mix_g": ln_mix_g, "ln_mix_b": ln_mix_b,
            "w_router": w_router, "b_router": b_router,
            "w_exp_in": w_exp_in, "b_exp_in": b_exp_in,
            "w_exp_out": w_exp_out, "b_exp_out": b_exp_out,
            "ln_ffn_g": ln_ffn_g, "ln_ffn_b": ln_ffn_b}


def reference(x, w_ret_in, w_ret_out, w_swa_in, b_swa_in, w_swa_out, b_swa_out, swa_sinks,
              ln_mix_g, ln_mix_b, w_router, b_router, w_exp_in, b_exp_in, w_exp_out, b_exp_out,
              ln_ffn_g, ln_ffn_b):
    for i in range(DEPTH):
        j = i // N_MIXERS
        if i % N_MIXERS == 0:
            m = retention(x, w_ret_in[j], w_ret_out[j])
        else:
            m = sliding_window_attention(x, w_swa_in[j], b_swa_in[j], w_swa_out[j], b_swa_out[j], swa_sinks[j])
        x = layer_norm(DEEPNORM_ALPHA * x + m, ln_mix_g[i], ln_mix_b[i])
        f = moe(x, w_router[i], b_router[i], w_exp_in[i], b_exp_in[i], w_exp_out[i], b_exp_out[i])
        x = layer_norm(DEEPNORM_ALPHA * x + f, ln_ffn_g[i], ln_ffn_b[i])
    return x
```

```python
import functools
import math

import jax
import jax.numpy as jnp
from jax import lax
from jax.experimental import pallas as pl
from jax.experimental.pallas import tpu as pltpu

F32 = jnp.float32
BF16 = jnp.bfloat16
I32 = jnp.int32
U32 = jnp.uint32

D_MODEL = 1024
DEPTH = 2
RET_HEADS = 4
RET_QK_DIM = 256
RET_V_DIM = 512
RET_CHUNK = 128
RET_QK_ALL = RET_HEADS * RET_QK_DIM
RET_V_ALL = RET_HEADS * RET_V_DIM
RET_IN = 2 * RET_QK_ALL + 2 * RET_V_ALL
SWA_Q_HEADS = 16
SWA_KV_HEADS = 2
SWA_GROUP = 8
SWA_HEAD_DIM = 64
SWA_BLOCK = 128
WINDOW = 128
SWA_Q_ALL = SWA_Q_HEADS * SWA_HEAD_DIM
SWA_KV_ALL = 2 * SWA_KV_HEADS * SWA_HEAD_DIM
SWA_IN = SWA_Q_ALL + SWA_KV_ALL
N_EXPERTS = 32
TOP_K = 4
D_FF = 1024
SWIGLU_ALPHA = 1.702
SWIGLU_LIMIT = 7.0
LN_EPS = 1e-5
GN_EPS = 1e-6
DEEPNORM_ALPHA = (2 * DEPTH) ** 0.25

V7X_VMEM_BYTES = 64 * 1024 * 1024
VMEM_CAP = V7X_VMEM_BYTES - 8 * 1024 * 1024
ROW_TILE = 512
PROJ_N_CHUNK = 1536
TOK_BLOCK = 256
SLOT_BLOCK = 256
PACK_W = D_MODEL // 2

NT_DIMS = (((1,), (1,)), ((), ()))
TN_DIMS = (((0,), (0,)), ((), ()))


def _vmem_limit(*nbytes):
    est = int(sum(nbytes) * 1.2) + (4 << 20)
    return min(max(est, 16 << 20), VMEM_CAP)


def _params(vmem, n_grid):
    return pltpu.CompilerParams(dimension_semantics=("arbitrary",) * n_grid,
                                vmem_limit_bytes=vmem)


def _layer_norm_rows(y, g, b):
    mu = jnp.mean(y, axis=-1, keepdims=True)
    yc = y - mu
    var = jnp.mean(yc * yc, axis=-1, keepdims=True)
    return yc * lax.rsqrt(var + LN_EPS) * g + b


def _pack_rows(x):
    lo = x[:, :PACK_W].astype(BF16).astype(F32)
    hi = x[:, PACK_W:].astype(BF16).astype(F32)
    return pltpu.bitcast(hi, U32) | (pltpu.bitcast(lo, U32) >> 16)


def _unpack_rows(w):
    lo = pltpu.bitcast(w << 16, F32).astype(BF16)
    hi = pltpu.bitcast(w & jnp.uint32(0xFFFF0000), F32).astype(BF16)
    return lo, hi


def _proj_kernel(x_ref, w_ref, b_ref, o_ref, *, n_chunk):
    xb = x_ref[...].astype(BF16)
    n_out = o_ref.shape[1]
    for c in range(0, n_out, n_chunk):
        acc = jnp.dot(xb, w_ref[:, c:c + n_chunk], preferred_element_type=F32)
        o_ref[:, c:c + n_chunk] = (acc + b_ref[:, c:c + n_chunk]).astype(o_ref.dtype)


def _proj(x, w_bf16, bias, n_chunk):
    t, d = x.shape
    n = w_bf16.shape[1]
    tm = ROW_TILE
    vmem = _vmem_limit(2 * tm * d * 4, 2 * d * n * 2, 2 * tm * n * 2, tm * n_chunk * 4 * 2, tm * d * 2)
    return pl.pallas_call(
        functools.partial(_proj_kernel, n_chunk=n_chunk),
        grid=(t // tm,),
        in_specs=[pl.BlockSpec((tm, d), lambda i: (i, 0)),
                  pl.BlockSpec((d, n), lambda i: (0, 0)),
                  pl.BlockSpec((1, n), lambda i: (0, 0))],
        out_specs=pl.BlockSpec((tm, n), lambda i: (i, 0)),
        out_shape=jax.ShapeDtypeStruct((t, n), BF16),
        compiler_params=_params(vmem, 1),
    )(x, w_bf16, bias)


def _ret_gammas():
    return [1.0 - 2.0 ** (-5.0 - h) for h in range(RET_HEADS)]


def _ret_kernel(q_ref, k_ref, v_ref, g_ref, dm_ref, qd_ref, kd_ref, o_ref, state_ref):
    @pl.when(pl.program_id(1) == 0)
    def _():
        state_ref[...] = jnp.zeros_like(state_ref)

    gammas = _ret_gammas()
    for h in range(RET_HEADS):
        qh = q_ref[:, h * RET_QK_DIM:(h + 1) * RET_QK_DIM]
        kh = k_ref[:, h * RET_QK_DIM:(h + 1) * RET_QK_DIM]
        vh = v_ref[:, h * RET_V_DIM:(h + 1) * RET_V_DIM]
        s = lax.dot_general(qh, kh, NT_DIMS, preferred_element_type=F32) * dm_ref[h]
        o = jnp.dot(s.astype(BF16), vh, preferred_element_type=F32)
        st = state_ref[h]
        o = o + jnp.dot(qh, st.astype(BF16), preferred_element_type=F32) * qd_ref[h]
        kdec = (kh.astype(F32) * kd_ref[h]).astype(BF16)
        upd = lax.dot_general(kdec, vh, TN_DIMS, preferred_element_type=F32)
        state_ref[h] = st * (gammas[h] ** RET_CHUNK) + upd
        mu = jnp.mean(o, axis=-1, keepdims=True)
        oc = o - mu
        var = jnp.mean(oc * oc, axis=-1, keepdims=True)
        on = oc * lax.rsqrt(var + GN_EPS)
        gh = g_ref[:, h * RET_V_DIM:(h + 1) * RET_V_DIM].astype(F32)
        gate = gh / (1.0 + jnp.exp(-gh))
        o_ref[:, h * RET_V_DIM:(h + 1) * RET_V_DIM] = (gate * on).astype(o_ref.dtype)


def _retention_core(proj, batch, seq):
    c = RET_CHUNK
    nc = seq // c
    log_g = jnp.log(jnp.asarray(_ret_gammas(), F32))
    pos = jnp.arange(c, dtype=F32)
    rel = pos[:, None] - pos[None, :]
    scale = RET_QK_DIM ** -0.5
    dm = jnp.where(rel >= 0, jnp.exp(log_g[:, None, None] * jnp.maximum(rel, 0.0)), 0.0) * scale
    qd = jnp.exp(log_g[:, None] * (pos + 1.0))[..., None]
    kd = jnp.exp(log_g[:, None] * (c - 1.0 - pos))[..., None] * scale
    t = batch * seq
    row = lambda b, i: b * nc + i
    const3 = lambda b, i: (0, 0, 0)
    vmem = _vmem_limit(2 * c * RET_IN * 2, 2 * c * RET_V_ALL * 2, RET_HEADS * RET_QK_DIM * RET_V_DIM * 4 * 2,
                       8 << 20)
    return pl.pallas_call(
        _ret_kernel,
        grid=(batch, nc),
        in_specs=[pl.BlockSpec((c, RET_QK_ALL), lambda b, i: (row(b, i), 0)),
                  pl.BlockSpec((c, RET_QK_ALL), lambda b, i: (row(b, i), 1)),
                  pl.BlockSpec((c, RET_V_ALL), lambda b, i: (row(b, i), 1)),
                  pl.BlockSpec((c, RET_V_ALL), lambda b, i: (row(b, i), 2)),
                  pl.BlockSpec((RET_HEADS, c, c), const3),
                  pl.BlockSpec((RET_HEADS, c, 1), const3),
                  pl.BlockSpec((RET_HEADS, c, 1), const3)],
        out_specs=pl.BlockSpec((c, RET_V_ALL), lambda b, i: (row(b, i), 0)),
        out_shape=jax.ShapeDtypeStruct((t, RET_V_ALL), BF16),
        scratch_shapes=[pltpu.VMEM((RET_HEADS, RET_QK_DIM, RET_V_DIM), F32)],
        compiler_params=_params(vmem, 2),
    )(proj, proj, proj, proj, dm, qd, kd)


def _swa_kernel(sink_ref, q_ref, kvp_ref, kvc_ref, o_ref):
    n = pl.program_id(1)
    c = SWA_BLOCK
    hd = SWA_HEAD_DIM
    qi = lax.broadcasted_iota(I32, (c, 2 * c), 0)
    kj = lax.broadcasted_iota(I32, (c, 2 * c), 1)
    dist = qi + c - kj
    valid = (dist >= 0) & (dist < WINDOW) & (n * c + kj - c >= 0)
    distf = dist.astype(F32)
    outs = []
    for j in range(SWA_KV_HEADS):
        kcat = jnp.concatenate([kvp_ref[:, j * hd:(j + 1) * hd], kvc_ref[:, j * hd:(j + 1) * hd]], axis=0)
        v0 = SWA_KV_HEADS * hd + j * hd
        vcat = jnp.concatenate([kvp_ref[:, v0:v0 + hd], kvc_ref[:, v0:v0 + hd]], axis=0)
        for g in range(SWA_GROUP):
            h = j * SWA_GROUP + g
            slope = 2.0 ** (-8.0 * (h + 1) / SWA_Q_HEADS)
            qh = q_ref[:, h * hd:(h + 1) * hd]
            s = lax.dot_general(qh, kcat, NT_DIMS, preferred_element_type=F32) * (hd ** -0.5)
            s = jnp.where(valid, s - slope * distf, -jnp.inf)
            sink = sink_ref[h]
            m = jnp.maximum(jnp.max(s, axis=-1, keepdims=True), sink)
            p = jnp.exp(s - m)
            den = jnp.sum(p, axis=-1, keepdims=True) + jnp.exp(sink - m)
            o = jnp.dot(p.astype(BF16), vcat, preferred_element_type=F32) / den
            outs.append(o)
    o_ref[...] = jnp.concatenate(outs, axis=-1).astype(o_ref.dtype)


def _swa_core(proj, sinks, batch, seq):
    c = SWA_BLOCK
    nb = seq // c
    t = batch * seq
    kv_col = SWA_Q_ALL // SWA_KV_ALL
    vmem = _vmem_limit(2 * c * SWA_IN * 2 * 2, 16 << 20)
    return pl.pallas_call(
        _swa_kernel,
        grid=(batch, nb),
        in_specs=[pl.BlockSpec(memory_space=pltpu.SMEM),
                  pl.BlockSpec((c, SWA_Q_ALL), lambda b, n: (b * nb + n, 0)),
                  pl.BlockSpec((c, SWA_KV_ALL), lambda b, n: (b * nb + jnp.maximum(n - 1, 0), kv_col)),
                  pl.BlockSpec((c, SWA_KV_ALL), lambda b, n: (b * nb + n, kv_col))],
        out_specs=pl.BlockSpec((c, SWA_Q_ALL), lambda b, n: (b * nb + n, 0)),
        out_shape=jax.ShapeDtypeStruct((t, SWA_Q_ALL), BF16),
        compiler_params=_params(vmem, 2),
    )(sinks, proj, proj, proj)


def _outproj_ln_kernel(a_ref, w_ref, b_ref, x_ref, g_ref, beta_ref, o_ref, p_ref):
    m = jnp.dot(a_ref[...], w_ref[...], preferred_element_type=F32) + b_ref[...]
    out = _layer_norm_rows(DEEPNORM_ALPHA * x_ref[...] + m, g_ref[...], beta_ref[...])
    o_ref[...] = out
    p_ref[...] = _pack_rows(out)


def _outproj_ln(a, w_bf16, bias, x, g, beta):
    t, kin = a.shape
    d = D_MODEL
    tm = ROW_TILE
    vec = pl.BlockSpec((1, d), lambda i: (0, 0))
    vmem = _vmem_limit(2 * tm * kin * 2, 2 * kin * d * 2, 4 * tm * d * 4, 2 * tm * PACK_W * 4, 4 * tm * d * 4)
    return pl.pallas_call(
        _outproj_ln_kernel,
        grid=(t // tm,),
        in_specs=[pl.BlockSpec((tm, kin), lambda i: (i, 0)),
                  pl.BlockSpec((kin, d), lambda i: (0, 0)),
                  vec,
                  pl.BlockSpec((tm, d), lambda i: (i, 0)),
                  vec, vec],
        out_specs=[pl.BlockSpec((tm, d), lambda i: (i, 0)),
                   pl.BlockSpec((tm, PACK_W), lambda i: (i, 0))],
        out_shape=[jax.ShapeDtypeStruct((t, d), F32),
                   jax.ShapeDtypeStruct((t, PACK_W), U32)],
        compiler_params=_params(vmem, 1),
    )(a, w_bf16, bias, x, g, beta)


def _router_kernel(x_ref, wt_ref, b_ref, e_ref, r_ref, gate_ref, cnt_ref, carry_ref):
    @pl.when(pl.program_id(0) == 0)
    def _():
        carry_ref[...] = jnp.zeros_like(carry_ref)

    tb = x_ref.shape[0]
    x = x_ref[...]
    wt = wt_ref[...]
    xh = x.astype(BF16)
    xl = (x - xh.astype(F32)).astype(BF16)
    wh = wt.astype(BF16)
    wl = (wt - wh.astype(F32)).astype(BF16)
    logits = (lax.dot_general(wh, xh, NT_DIMS, preferred_element_type=F32)
              + lax.dot_general(wh, xl, NT_DIMS, preferred_element_type=F32)
              + lax.dot_general(wl, xh, NT_DIMS, preferred_element_type=F32)
              + b_ref[...])

    eidx = lax.broadcasted_iota(I32, (N_EXPERTS, tb), 0).astype(F32)
    work = logits
    sels, vals, idxs = [], [], []
    for _ in range(TOP_K):
        m = jnp.max(work, axis=0, keepdims=True)
        idx = jnp.min(jnp.where(work == m, eidx, float(N_EXPERTS)), axis=0, keepdims=True)
        sel = eidx == idx
        sels.append(sel)
        vals.append(m)
        idxs.append(idx)
        work = jnp.where(sel, -jnp.inf, work)

    exps = [jnp.exp(v - vals[0]) for v in vals]
    den = exps[0] + exps[1] + exps[2] + exps[3]
    gates = [e / den for e in exps]

    mask = jnp.zeros((N_EXPERTS, tb), F32)
    for sel in sels:
        mask = mask + jnp.where(sel, 1.0, 0.0)
    ti = lax.broadcasted_iota(I32, (tb, tb), 0)
    tj = lax.broadcasted_iota(I32, (tb, tb), 1)
    upper = jnp.where(ti < tj, 1.0, 0.0).astype(BF16)
    carry = carry_ref[...]
    rank_all = jnp.dot(mask.astype(BF16), upper, preferred_element_type=F32) + carry[:, :1]
    ranks = [jnp.sum(jnp.where(sel, rank_all, 0.0), axis=0, keepdims=True) for sel in sels]

    e_ref[0] = jnp.concatenate(idxs, axis=0).astype(I32)
    r_ref[0] = jnp.concatenate(ranks, axis=0).astype(I32)
    gate_ref[0] = jnp.concatenate(gates, axis=0)
    new_carry = carry + jnp.sum(mask, axis=1, keepdims=True)
    carry_ref[...] = new_carry
    cnt_ref[...] = new_carry


def _router(x, w_router_t, b_router_col):
    t, d = x.shape
    tb = TOK_BLOCK
    nblk = t // tb
    blk3 = pl.BlockSpec((1, TOP_K, tb), lambda i: (i, 0, 0))
    vmem = _vmem_limit(2 * tb * d * 4, 2 * N_EXPERTS * d * 4, 8 << 20)
    return pl.pallas_call(
        _router_kernel,
        grid=(nblk,),
        in_specs=[pl.BlockSpec((tb, d), lambda i: (i, 0)),
                  pl.BlockSpec((N_EXPERTS, d), lambda i: (0, 0)),
                  pl.BlockSpec((N_EXPERTS, 1), lambda i: (0, 0))],
        out_specs=[blk3, blk3, blk3, pl.BlockSpec((N_EXPERTS, 128), lambda i: (0, 0))],
        out_shape=[jax.ShapeDtypeStruct((nblk, TOP_K, tb), I32),
                   jax.ShapeDtypeStruct((nblk, TOP_K, tb), I32),
                   jax.ShapeDtypeStruct((nblk, TOP_K, tb), F32),
                   jax.ShapeDtypeStruct((N_EXPERTS, 128), F32)],
        scratch_shapes=[pltpu.VMEM((N_EXPERTS, 128), F32)],
        compiler_params=_params(vmem, 1),
    )(x, w_router_t, b_router_col)


def _dest_kernel(start_ref, e_ref, r_ref, d_ref):
    e = e_ref[...]
    acc = r_ref[...]
    for ex in range(N_EXPERTS):
        acc = acc + jnp.where(e == ex, start_ref[ex], 0)
    d_ref[...] = acc


def _dest_slots(start_padded, e_idx, rank):
    full = pl.BlockSpec(e_idx.shape, lambda: (0, 0, 0))
    return pl.pallas_call(
        _dest_kernel,
        in_specs=[pl.BlockSpec(memory_space=pltpu.SMEM), full, full],
        out_specs=full,
        out_shape=jax.ShapeDtypeStruct(e_idx.shape, I32),
    )(start_padded, e_idx, rank)


def _dispatch_kernel(dest_hbm, xp_hbm, xs_in_hbm, xs_hbm, dest_smem, idx_sem, row_sem):
    del xs_in_hbm
    i = pl.program_id(0)
    tb = dest_smem.shape[1]
    idx_cp = pltpu.make_async_copy(dest_hbm.at[i], dest_smem, idx_sem)
    idx_cp.start()
    idx_cp.wait()
    base = i * tb

    def body(r, carry):
        src = xp_hbm.at[pl.ds(base + r, 1)]
        for k in range(TOP_K):
            pltpu.make_async_copy(src, xs_hbm.at[pl.ds(dest_smem[k, r], 1)], row_sem).start()
        return carry

    lax.fori_loop(0, tb, body, 0)
    pltpu.make_async_copy(xp_hbm.at[pl.ds(0, TOP_K * tb)], xs_hbm.at[pl.ds(0, TOP_K * tb)], row_sem).wait()


def _dispatch(dest, xp, n_slots):
    nblk, _, tb = dest.shape
    xs0 = jnp.zeros((n_slots, PACK_W), U32)
    any_spec = pl.BlockSpec(memory_space=pl.ANY)
    return pl.pallas_call(
        _dispatch_kernel,
        grid=(nblk,),
        in_specs=[any_spec, any_spec, any_spec],
        out_specs=any_spec,
        out_shape=jax.ShapeDtypeStruct((n_slots, PACK_W), U32),
        scratch_shapes=[pltpu.SMEM((TOP_K, tb), I32),
                        pltpu.SemaphoreType.DMA(()),
                        pltpu.SemaphoreType.DMA(())],
        input_output_aliases={2: 0},
        compiler_params=pltpu.CompilerParams(dimension_semantics=("arbitrary",), has_side_effects=True),
    )(dest, xp, xs0)


def _expert_kernel(be_ref, nu_ref, xs_ref, w1_ref, b1_ref, w2_ref, b2_ref, y_ref):
    i = pl.program_id(0)

    @pl.when(i < nu_ref[0])
    def _():
        lo, hi = _unpack_rows(xs_ref[...])
        h = (jnp.dot(lo, w1_ref[0, :PACK_W, :], preferred_element_type=F32)
             + jnp.dot(hi, w1_ref[0, PACK_W:, :], preferred_element_type=F32)
             + b1_ref[0])
        glu = jnp.minimum(h[:, :D_FF], SWIGLU_LIMIT)
        lin = jnp.clip(h[:, D_FF:], -SWIGLU_LIMIT, SWIGLU_LIMIT)
        act = glu / (1.0 + jnp.exp(-SWIGLU_ALPHA * glu)) * (lin + 1.0)
        y_ref[...] = jnp.dot(act.astype(BF16), w2_ref[0], preferred_element_type=F32) + b2_ref[0]

    @pl.when(i >= nu_ref[0])
    def _():
        y_ref[...] = jnp.zeros_like(y_ref)


def _experts(block_expert, n_used, xs, w1, b1, w2, b2):
    n_slots = xs.shape[0]
    bs = SLOT_BLOCK
    d = D_MODEL
    vmem = _vmem_limit(2 * bs * PACK_W * 4, 2 * d * 2 * D_FF * 2, 2 * D_FF * d * 2, 2 * bs * d * 4,
                       3 * bs * 2 * D_FF * 4)
    grid_spec = pltpu.PrefetchScalarGridSpec(
        num_scalar_prefetch=2,
        grid=(n_slots // bs,),
        in_specs=[pl.BlockSpec((bs, PACK_W), lambda i, be, nu: (i, 0)),
                  pl.BlockSpec((1, d, 2 * D_FF), lambda i, be, nu: (be[i], 0, 0)),
                  pl.BlockSpec((1, 1, 2 * D_FF), lambda i, be, nu: (be[i], 0, 0)),
                  pl.BlockSpec((1, D_FF, d), lambda i, be, nu: (be[i], 0, 0)),
                  pl.BlockSpec((1, 1, d), lambda i, be, nu: (be[i], 0, 0))],
        out_specs=pl.BlockSpec((bs, d), lambda i, be, nu: (i, 0)),
    )
    return pl.pallas_call(
        _expert_kernel,
        grid_spec=grid_spec,
        out_shape=jax.ShapeDtypeStruct((n_slots, d), F32),
        compiler_params=_params(vmem, 1),
    )(block_expert, n_used, xs, w1, b1, w2, b2)


def _combine_kernel(dest_hbm, y_hbm, gates_ref, x_ref, g_ref, beta_ref, o_ref, dest_smem, buf, idx_sem, row_sem):
    i = pl.program_id(0)
    tb = dest_smem.shape[1]
    idx_cp = pltpu.make_async_copy(dest_hbm.at[i], dest_smem, idx_sem)
    idx_cp.start()
    idx_cp.wait()

    def body(r, carry):
        for k in range(TOP_K):
            pltpu.make_async_copy(y_hbm.at[pl.ds(dest_smem[k, r], 1)], buf.at[k, pl.ds(r, 1)], row_sem).start()
        return carry

    lax.fori_loop(0, tb, body, 0)
    for k in range(TOP_K):
        pltpu.make_async_copy(y_hbm.at[pl.ds(0, tb)], buf.at[k], row_sem).wait()

    gates = gates_ref[...]
    f = gates[:, 0:1] * buf[0]
    for k in range(1, TOP_K):
        f = f + gates[:, k:k + 1] * buf[k]
    o_ref[...] = _layer_norm_rows(DEEPNORM_ALPHA * x_ref[...] + f, g_ref[...], beta_ref[...])


def _combine_ln(dest, y, gates_tok, x, g, beta):
    nblk, _, tb = dest.shape
    t, d = x.shape
    any_spec = pl.BlockSpec(memory_space=pl.ANY)
    vec = pl.BlockSpec((1, d), lambda i: (0, 0))
    vmem = _vmem_limit(TOP_K * tb * d * 4, 4 * tb * d * 4, 4 * tb * d * 4)
    return pl.pallas_call(
        _combine_kernel,
        grid=(nblk,),
        in_specs=[any_spec, any_spec,
                  pl.BlockSpec((tb, TOP_K), lambda i: (i, 0)),
                  pl.BlockSpec((tb, d), lambda i: (i, 0)),
                  vec, vec],
        out_specs=pl.BlockSpec((tb, d), lambda i: (i, 0)),
        out_shape=jax.ShapeDtypeStruct((t, d), F32),
        scratch_shapes=[pltpu.SMEM((TOP_K, tb), I32),
                        pltpu.VMEM((TOP_K, tb, d), F32),
                        pltpu.SemaphoreType.DMA(()),
                        pltpu.SemaphoreType.DMA(())],
        compiler_params=_params(vmem, 1),
    )(dest, y, gates_tok, x, g, beta)


def _moe_ln(x, xp, w_router, b_router, w1, b1, w2, b2, g, beta):
    t, d = x.shape
    bs = SLOT_BLOCK
    n_blocks = (t * TOP_K) // bs + N_EXPERTS
    n_slots = n_blocks * bs

    e_idx, rank, gates, cnt = _router(x, w_router.T, b_router.reshape(N_EXPERTS, 1))
    counts = cnt[:, 0].astype(I32)
    padded = (counts + bs - 1) // bs * bs
    padded_end = jnp.cumsum(padded)
    start_padded = (padded_end - padded).astype(I32)
    block_expert = jnp.minimum(
        jnp.searchsorted(padded_end, jnp.arange(n_blocks, dtype=I32) * bs, side='right'),
        N_EXPERTS - 1).astype(I32)
    n_used = (padded_end[-1:] // bs).astype(I32)

    dest = _dest_slots(start_padded, e_idx, rank)
    xs = _dispatch(dest, xp, n_slots)
    y = _experts(block_expert, n_used, xs, w1, b1.reshape(N_EXPERTS, 1, 2 * D_FF), w2,
                 b2.reshape(N_EXPERTS, 1, d))
    gates_tok = gates.transpose(0, 2, 1).reshape(t, TOP_K)
    return _combine_ln(dest, y, gates_tok, x, g, beta)


def kernel(x, w_ret_in, w_ret_out, w_swa_in, b_swa_in, w_swa_out, b_swa_out, swa_sinks, ln_mix_g, ln_mix_b,
           w_router, b_router, w_exp_in, b_exp_in, w_exp_out, b_exp_out, ln_ffn_g, ln_ffn_b):
    batch, seq, d = x.shape
    t = batch * seq
    xt = x.reshape(t, d)
    row = lambda v: v.reshape(1, -1)
    zeros_d = jnp.zeros((1, d), F32)
    for i in range(DEPTH):
        j = i // 2
        if i % 2 == 0:
            proj = _proj(xt, w_ret_in[j].astype(BF16), jnp.zeros((1, RET_IN), F32), PROJ_N_CHUNK)
            mixed = _retention_core(proj, batch, seq)
            w_out, b_out = w_ret_out[j].astype(BF16), zeros_d
        else:
            proj = _proj(xt, w_swa_in[j].astype(BF16), row(b_swa_in[j]), SWA_IN)
            mixed = _swa_core(proj, swa_sinks[j], batch, seq)
            w_out, b_out = w_swa_out[j].astype(BF16), row(b_swa_out[j])
        xt, xp = _outproj_ln(mixed, w_out, b_out, xt, row(ln_mix_g[i]), row(ln_mix_b[i]))
        xt = _moe_ln(xt, xp, w_router[i], b_router[i], w_exp_in[i].astype(BF16), b_exp_in[i],
                     w_exp_out[i].astype(BF16), b_exp_out[i], row(ln_ffn_g[i]), row(ln_ffn_b[i]))
    return xt.reshape(batch, seq, d)
```

```python
import functools
import math

import jax
import jax.numpy as jnp
from jax import lax
from jax.experimental import pallas as pl
from jax.experimental.pallas import tpu as pltpu

F32 = jnp.float32
BF16 = jnp.bfloat16
I32 = jnp.int32
U32 = jnp.uint32

D_MODEL = 1024
DEPTH = 2
RET_HEADS = 4
RET_QK_DIM = 256
RET_V_DIM = 512
RET_CHUNK = 128
RET_QK_ALL = RET_HEADS * RET_QK_DIM
RET_V_ALL = RET_HEADS * RET_V_DIM
RET_IN = 2 * RET_QK_ALL + 2 * RET_V_ALL
SWA_Q_HEADS = 16
SWA_KV_HEADS = 2
SWA_GROUP = 8
SWA_HEAD_DIM = 64
SWA_BLOCK = 128
WINDOW = 128
SWA_Q_ALL = SWA_Q_HEADS * SWA_HEAD_DIM
SWA_KV_ALL = 2 * SWA_KV_HEADS * SWA_HEAD_DIM
SWA_IN = SWA_Q_ALL + SWA_KV_ALL
N_EXPERTS = 32
TOP_K = 4
D_FF = 1024
SWIGLU_ALPHA = 1.702
SWIGLU_LIMIT = 7.0
LN_EPS = 1e-5
GN_EPS = 1e-6
DEEPNORM_ALPHA = (2 * DEPTH) ** 0.25

V7X_VMEM_BYTES = 64 * 1024 * 1024
VMEM_CAP = V7X_VMEM_BYTES - 8 * 1024 * 1024
ROW_TILE = 512
PROJ_N_CHUNK = 1536
TOK_BLOCK = 256
SLOT_BLOCK = 256
ISSUE_UNROLL = 8
SUBLANES = 8
FILL_BITS = tuple(1 << b for b in range(SLOT_BLOCK.bit_length() - 2, SUBLANES.bit_length() - 2, -1))

NT_DIMS = (((1,), (1,)), ((), ()))
TN_DIMS = (((0,), (0,)), ((), ()))


def _vmem_limit(*nbytes):
    est = int(sum(nbytes) * 1.2) + (4 << 20)
    return min(max(est, 16 << 20), VMEM_CAP)


def _params(vmem, n_grid):
    return pltpu.CompilerParams(dimension_semantics=("arbitrary",) * n_grid,
                                vmem_limit_bytes=vmem)


def _layer_norm_rows(y, g, b):
    mu = jnp.mean(y, axis=-1, keepdims=True)
    yc = y - mu
    var = jnp.mean(yc * yc, axis=-1, keepdims=True)
    return yc * lax.rsqrt(var + LN_EPS) * g + b


def _proj_kernel(x_ref, w_ref, b_ref, o_ref, *, n_chunk):
    xb = x_ref[...].astype(BF16)
    n_out = o_ref.shape[1]
    for c in range(0, n_out, n_chunk):
        acc = jnp.dot(xb, w_ref[:, c:c + n_chunk], preferred_element_type=F32)
        o_ref[:, c:c + n_chunk] = (acc + b_ref[:, c:c + n_chunk]).astype(o_ref.dtype)


def _proj(x, w_bf16, bias, n_chunk):
    t, d = x.shape
    n = w_bf16.shape[1]
    tm = ROW_TILE
    vmem = _vmem_limit(2 * tm * d * 4, 2 * d * n * 2, 2 * tm * n * 2, tm * n_chunk * 4 * 2, tm * d * 2)
    return pl.pallas_call(
        functools.partial(_proj_kernel, n_chunk=n_chunk),
        grid=(t // tm,),
        in_specs=[pl.BlockSpec((tm, d), lambda i: (i, 0)),
                  pl.BlockSpec((d, n), lambda i: (0, 0)),
                  pl.BlockSpec((1, n), lambda i: (0, 0))],
        out_specs=pl.BlockSpec((tm, n), lambda i: (i, 0)),
        out_shape=jax.ShapeDtypeStruct((t, n), BF16),
        compiler_params=_params(vmem, 1),
    )(x, w_bf16, bias)


def _ret_gammas():
    return [1.0 - 2.0 ** (-5.0 - h) for h in range(RET_HEADS)]


def _ret_kernel(q_ref, k_ref, v_ref, g_ref, dm_ref, qd_ref, kd_ref, o_ref, state_ref):
    @pl.when(pl.program_id(1) == 0)
    def _():
        state_ref[...] = jnp.zeros_like(state_ref)

    gammas = _ret_gammas()
    for h in range(RET_HEADS):
        qh = q_ref[:, h * RET_QK_DIM:(h + 1) * RET_QK_DIM]
        kh = k_ref[:, h * RET_QK_DIM:(h + 1) * RET_QK_DIM]
        vh = v_ref[:, h * RET_V_DIM:(h + 1) * RET_V_DIM]
        s = lax.dot_general(qh, kh, NT_DIMS, preferred_element_type=F32) * dm_ref[h]
        o = jnp.dot(s.astype(BF16), vh, preferred_element_type=F32)
        st = state_ref[h]
        o = o + jnp.dot(qh, st.astype(BF16), preferred_element_type=F32) * qd_ref[h]
        kdec = (kh.astype(F32) * kd_ref[h]).astype(BF16)
        upd = lax.dot_general(kdec, vh, TN_DIMS, preferred_element_type=F32)
        state_ref[h] = st * (gammas[h] ** RET_CHUNK) + upd
        mu = jnp.mean(o, axis=-1, keepdims=True)
        oc = o - mu
        var = jnp.mean(oc * oc, axis=-1, keepdims=True)
        on = oc * lax.rsqrt(var + GN_EPS)
        gh = g_ref[:, h * RET_V_DIM:(h + 1) * RET_V_DIM].astype(F32)
        gate = gh / (1.0 + jnp.exp(-gh))
        o_ref[:, h * RET_V_DIM:(h + 1) * RET_V_DIM] = (gate * on).astype(o_ref.dtype)


def _retention_core(proj, batch, seq):
    c = RET_CHUNK
    nc = seq // c
    log_g = jnp.log(jnp.asarray(_ret_gammas(), F32))
    pos = jnp.arange(c, dtype=F32)
    rel = pos[:, None] - pos[None, :]
    scale = RET_QK_DIM ** -0.5
    dm = jnp.where(rel >= 0, jnp.exp(log_g[:, None, None] * jnp.maximum(rel, 0.0)), 0.0) * scale
    qd = jnp.exp(log_g[:, None] * (pos + 1.0))[..., None]
    kd = jnp.exp(log_g[:, None] * (c - 1.0 - pos))[..., None] * scale
    t = batch * seq
    row = lambda b, i: b * nc + i
    const3 = lambda b, i: (0, 0, 0)
    vmem = _vmem_limit(2 * c * RET_IN * 2, 2 * c * RET_V_ALL * 2, RET_HEADS * RET_QK_DIM * RET_V_DIM * 4 * 2,
                       8 << 20)
    return pl.pallas_call(
        _ret_kernel,
        grid=(batch, nc),
        in_specs=[pl.BlockSpec((c, RET_QK_ALL), lambda b, i: (row(b, i), 0)),
                  pl.BlockSpec((c, RET_QK_ALL), lambda b, i: (row(b, i), 1)),
                  pl.BlockSpec((c, RET_V_ALL), lambda b, i: (row(b, i), 1)),
                  pl.BlockSpec((c, RET_V_ALL), lambda b, i: (row(b, i), 2)),
                  pl.BlockSpec((RET_HEADS, c, c), const3),
                  pl.BlockSpec((RET_HEADS, c, 1), const3),
                  pl.BlockSpec((RET_HEADS, c, 1), const3)],
        out_specs=pl.BlockSpec((c, RET_V_ALL), lambda b, i: (row(b, i), 0)),
        out_shape=jax.ShapeDtypeStruct((t, RET_V_ALL), BF16),
        scratch_shapes=[pltpu.VMEM((RET_HEADS, RET_QK_DIM, RET_V_DIM), F32)],
        compiler_params=_params(vmem, 2),
    )(proj, proj, proj, proj, dm, qd, kd)


def _swa_kernel(sink_ref, q_ref, kvp_ref, kvc_ref, o_ref):
    n = pl.program_id(1)
    c = SWA_BLOCK
    hd = SWA_HEAD_DIM
    qi = lax.broadcasted_iota(I32, (c, 2 * c), 0)
    kj = lax.broadcasted_iota(I32, (c, 2 * c), 1)
    dist = qi + c - kj
    valid = (dist >= 0) & (dist < WINDOW) & (n * c + kj - c >= 0)
    distf = dist.astype(F32)
    outs = []
    for j in range(SWA_KV_HEADS):
        kcat = jnp.concatenate([kvp_ref[:, j * hd:(j + 1) * hd], kvc_ref[:, j * hd:(j + 1) * hd]], axis=0)
        v0 = SWA_KV_HEADS * hd + j * hd
        vcat = jnp.concatenate([kvp_ref[:, v0:v0 + hd], kvc_ref[:, v0:v0 + hd]], axis=0)
        for g in range(SWA_GROUP):
            h = j * SWA_GROUP + g
            slope = 2.0 ** (-8.0 * (h + 1) / SWA_Q_HEADS)
            qh = q_ref[:, h * hd:(h + 1) * hd]
            s = lax.dot_general(qh, kcat, NT_DIMS, preferred_element_type=F32) * (hd ** -0.5)
            s = jnp.where(valid, s - slope * distf, -jnp.inf)
            sink = sink_ref[h]
            m = jnp.maximum(jnp.max(s, axis=-1, keepdims=True), sink)
            p = jnp.exp(s - m)
            den = jnp.sum(p, axis=-1, keepdims=True) + jnp.exp(sink - m)
            o = jnp.dot(p.astype(BF16), vcat, preferred_element_type=F32) / den
            outs.append(o)
    o_ref[...] = jnp.concatenate(outs, axis=-1).astype(o_ref.dtype)


def _swa_core(proj, sinks, batch, seq):
    c = SWA_BLOCK
    nb = seq // c
    t = batch * seq
    kv_col = SWA_Q_ALL // SWA_KV_ALL
    vmem = _vmem_limit(2 * c * SWA_IN * 2 * 2, 16 << 20)
    return pl.pallas_call(
        _swa_kernel,
        grid=(batch, nb),
        in_specs=[pl.BlockSpec(memory_space=pltpu.SMEM),
                  pl.BlockSpec((c, SWA_Q_ALL), lambda b, n: (b * nb + n, 0)),
                  pl.BlockSpec((c, SWA_KV_ALL), lambda b, n: (b * nb + jnp.maximum(n - 1, 0), kv_col)),
                  pl.BlockSpec((c, SWA_KV_ALL), lambda b, n: (b * nb + n, kv_col))],
        out_specs=pl.BlockSpec((c, SWA_Q_ALL), lambda b, n: (b * nb + n, 0)),
        out_shape=jax.ShapeDtypeStruct((t, SWA_Q_ALL), BF16),
        compiler_params=_params(vmem, 2),
    )(sinks, proj, proj, proj)


def _outproj_ln_kernel(a_ref, w_ref, b_ref, x_ref, g_ref, beta_ref, o_ref):
    m = jnp.dot(a_ref[...], w_ref[...], preferred_element_type=F32) + b_ref[...]
    o_ref[...] = _layer_norm_rows(DEEPNORM_ALPHA * x_ref[...] + m, g_ref[...], beta_ref[...])


def _outproj_ln(a, w_bf16, bias, x, g, beta):
    t, kin = a.shape
    d = D_MODEL
    tm = ROW_TILE
    vec = pl.BlockSpec((1, d), lambda i: (0, 0))
    vmem = _vmem_limit(2 * tm * kin * 2, 2 * kin * d * 2, 4 * tm * d * 4, 4 * tm * d * 4)
    return pl.pallas_call(
        _outproj_ln_kernel,
        grid=(t // tm,),
        in_specs=[pl.BlockSpec((tm, kin), lambda i: (i, 0)),
                  pl.BlockSpec((kin, d), lambda i: (0, 0)),
                  vec,
                  pl.BlockSpec((tm, d), lambda i: (i, 0)),
                  vec, vec],
        out_specs=pl.BlockSpec((tm, d), lambda i: (i, 0)),
        out_shape=jax.ShapeDtypeStruct((t, d), F32),
        compiler_params=_params(vmem, 1),
    )(a, w_bf16, bias, x, g, beta)


def _router_kernel(x_ref, wt_ref, b_ref, e_ref, r_ref, gate_ref, cnt_ref, carry_ref):
    @pl.when(pl.program_id(0) == 0)
    def _():
        carry_ref[...] = jnp.zeros_like(carry_ref)

    tb = x_ref.shape[0]
    x = x_ref[...]
    wt = wt_ref[...]
    xh = x.astype(BF16)
    xl = (x - xh.astype(F32)).astype(BF16)
    wh = wt.astype(BF16)
    wl = (wt - wh.astype(F32)).astype(BF16)
    logits = (lax.dot_general(wh, xh, NT_DIMS, preferred_element_type=F32)
              + lax.dot_general(wh, xl, NT_DIMS, preferred_element_type=F32)
              + lax.dot_general(wl, xh, NT_DIMS, preferred_element_type=F32)
              + b_ref[...])

    eidx = lax.broadcasted_iota(I32, (N_EXPERTS, tb), 0).astype(F32)
    work = logits
    sels, vals, idxs = [], [], []
    for _ in range(TOP_K):
        m = jnp.max(work, axis=0, keepdims=True)
        idx = jnp.min(jnp.where(work == m, eidx, float(N_EXPERTS)), axis=0, keepdims=True)
        sel = eidx == idx
        sels.append(sel)
        vals.append(m)
        idxs.append(idx)
        work = jnp.where(sel, -jnp.inf, work)

    exps = [jnp.exp(v - vals[0]) for v in vals]
    den = exps[0] + exps[1] + exps[2] + exps[3]
    gates = [e / den for e in exps]

    mask = jnp.zeros((N_EXPERTS, tb), F32)
    for sel in sels:
        mask = mask + jnp.where(sel, 1.0, 0.0)
    ti = lax.broadcasted_iota(I32, (tb, tb), 0)
    tj = lax.broadcasted_iota(I32, (tb, tb), 1)
    upper = jnp.where(ti < tj, 1.0, 0.0).astype(BF16)
    carry = carry_ref[...]
    rank_all = jnp.dot(mask.astype(BF16), upper, preferred_element_type=F32) + carry[:, :1]
    ranks = [jnp.sum(jnp.where(sel, rank_all, 0.0), axis=0, keepdims=True) for sel in sels]

    e_ref[0] = jnp.concatenate(idxs, axis=0).astype(I32)
    r_ref[0] = jnp.concatenate(ranks, axis=0).astype(I32)
    gate_ref[0] = jnp.concatenate(gates, axis=0)
    new_carry = carry + jnp.sum(mask, axis=1, keepdims=True)
    carry_ref[...] = new_carry
    cnt_ref[...] = new_carry


def _router(x, w_router_t, b_router_col):
    t, d = x.shape
    tb = TOK_BLOCK
    nblk = t // tb
    blk3 = pl.BlockSpec((1, TOP_K, tb), lambda i: (i, 0, 0))
    vmem = _vmem_limit(2 * tb * d * 4, 2 * N_EXPERTS * d * 4, 8 << 20)
    return pl.pallas_call(
        _router_kernel,
        grid=(nblk,),
        in_specs=[pl.BlockSpec((tb, d), lambda i: (i, 0)),
                  pl.BlockSpec((N_EXPERTS, d), lambda i: (0, 0)),
                  pl.BlockSpec((N_EXPERTS, 1), lambda i: (0, 0))],
        out_specs=[blk3, blk3, blk3, pl.BlockSpec((N_EXPERTS, 128), lambda i: (0, 0))],
        out_shape=[jax.ShapeDtypeStruct((nblk, TOP_K, tb), I32),
                   jax.ShapeDtypeStruct((nblk, TOP_K, tb), I32),
                   jax.ShapeDtypeStruct((nblk, TOP_K, tb), F32),
                   jax.ShapeDtypeStruct((N_EXPERTS, 128), F32)],
        scratch_shapes=[pltpu.VMEM((N_EXPERTS, 128), F32)],
        compiler_params=_params(vmem, 1),
    )(x, w_router_t, b_router_col)


def _dest_kernel(start_ref, e_ref, r_ref, d_ref):
    e = e_ref[...]
    acc = r_ref[...]
    for ex in range(N_EXPERTS):
        acc = acc + jnp.where(e == ex, start_ref[ex], 0)
    d_ref[...] = acc


def _dest_slots(start_padded, e_idx, rank):
    full = pl.BlockSpec(e_idx.shape, lambda: (0, 0, 0))
    return pl.pallas_call(
        _dest_kernel,
        in_specs=[pl.BlockSpec(memory_space=pltpu.SMEM), full, full],
        out_specs=full,
        out_shape=jax.ShapeDtypeStruct(e_idx.shape, I32),
    )(start_padded, e_idx, rank)


def _dispatch_kernel(fill_ref, dest_hbm, x_ref, xs_hbm, dest_smem, zeros_ref, idx_sem, row_sem, fill_sem):
    i = pl.program_id(0)
    tb = dest_smem.shape[1]

    @pl.when(i == 0)
    def _():
        zeros_ref[...] = jnp.zeros_like(zeros_ref)

        def fill_copies(e, act):
            first = fill_ref[0, e]
            head = fill_ref[1, e]
            rows = fill_ref[2, e]
            for u in range(SUBLANES - 1):
                @pl.when(u < head)
                def _():
                    act(pltpu.make_async_copy(zeros_ref.at[pl.ds(0, 1)], xs_hbm.at[pl.ds(first + u, 1)],
                                              fill_sem))
            off = first + head
            for bit in FILL_BITS:
                @pl.when((rows & bit) != 0)
                def _():
                    act(pltpu.make_async_copy(zeros_ref.at[pl.ds(0, bit)],
                                              xs_hbm.at[pl.ds(pl.multiple_of(off, SUBLANES), bit)], fill_sem))
                off = off + (rows & bit)

        def start_body(e, carry):
            fill_copies(e, lambda cp: cp.start())
            return carry

        def wait_body(e, carry):
            fill_copies(e, lambda cp: cp.wait())
            return carry

        lax.fori_loop(0, N_EXPERTS, start_body, 0)
        lax.fori_loop(0, N_EXPERTS, wait_body, 0)

        zrows = zeros_ref.shape[0]
        first_tail = fill_ref[0, N_EXPERTS] // zrows
        n_chunks = xs_hbm.shape[0] // zrows

        def tail_copy(j):
            return pltpu.make_async_copy(zeros_ref, xs_hbm.at[pl.ds(j * zrows, zrows)], fill_sem)

        def tail_start(j, carry):
            tail_copy(j).start()
            return carry

        def tail_wait(j, carry):
            tail_copy(j).wait()
            return carry

        lax.fori_loop(first_tail, n_chunks, tail_start, 0)
        lax.fori_loop(first_tail, n_chunks, tail_wait, 0)

    idx_cp = pltpu.make_async_copy(dest_hbm.at[i], dest_smem, idx_sem)
    idx_cp.start()
    idx_cp.wait()

    def body(rg, carry):
        for u in range(ISSUE_UNROLL):
            r = rg * ISSUE_UNROLL + u
            for k in range(TOP_K):
                pltpu.make_async_copy(x_ref.at[pl.ds(r, 1)], xs_hbm.at[pl.ds(dest_smem[k, r], 1)],
                                      row_sem).start()
        return carry

    lax.fori_loop(0, tb // ISSUE_UNROLL, body, 0)
    for k in range(TOP_K):
        pltpu.make_async_copy(x_ref, xs_hbm.at[pl.ds(0, tb)], row_sem).wait()


def _dispatch(fill, dest, x, n_slots):
    nblk, _, tb = dest.shape
    d = x.shape[1]
    any_spec = pl.BlockSpec(memory_space=pl.ANY)
    vmem = _vmem_limit(2 * tb * d * 4, (SLOT_BLOCK // 2) * d * 4)
    return pl.pallas_call(
        _dispatch_kernel,
        grid=(nblk,),
        in_specs=[pl.BlockSpec(memory_space=pltpu.SMEM), any_spec, pl.BlockSpec((tb, d), lambda i: (i, 0))],
        out_specs=any_spec,
        out_shape=jax.ShapeDtypeStruct((n_slots, d), F32),
        scratch_shapes=[pltpu.SMEM((TOP_K, tb), I32),
                        pltpu.VMEM((SLOT_BLOCK // 2, d), F32),
                        pltpu.SemaphoreType.DMA(()),
                        pltpu.SemaphoreType.DMA(()),
                        pltpu.SemaphoreType.DMA(())],
        compiler_params=pltpu.CompilerParams(dimension_semantics=("arbitrary",), has_side_effects=True,
                                             vmem_limit_bytes=vmem),
    )(fill, dest, x)


def _expert_kernel(be_ref, nu_ref, xs_ref, w1_ref, b1_ref, w2_ref, b2_ref, y_ref):
    i = pl.program_id(0)

    @pl.when(i < nu_ref[0])
    def _():
        h = jnp.dot(xs_ref[...].astype(BF16), w1_ref[0], preferred_element_type=F32) + b1_ref[0]
        glu = jnp.minimum(h[:, :D_FF], SWIGLU_LIMIT)
        lin = jnp.clip(h[:, D_FF:], -SWIGLU_LIMIT, SWIGLU_LIMIT)
        act = glu / (1.0 + jnp.exp(-SWIGLU_ALPHA * glu)) * (lin + 1.0)
        y_ref[...] = jnp.dot(act.astype(BF16), w2_ref[0], preferred_element_type=F32) + b2_ref[0]

    @pl.when(i >= nu_ref[0])
    def _():
        y_ref[...] = jnp.zeros_like(y_ref)


def _experts(layer, block_expert, n_used, xs, w1_all, b1_all, w2_all, b2_all):
    n_slots = xs.shape[0]
    bs = SLOT_BLOCK
    d = D_MODEL
    vmem = _vmem_limit(2 * bs * d * 4, 2 * d * 2 * D_FF * 2, 2 * D_FF * d * 2, 2 * bs * d * 4,
                       3 * bs * 2 * D_FF * 4)
    wmap = lambda i, be, nu: (layer, be[i], 0, 0)
    grid_spec = pltpu.PrefetchScalarGridSpec(
        num_scalar_prefetch=2,
        grid=(n_slots // bs,),
        in_specs=[pl.BlockSpec((bs, d), lambda i, be, nu: (i, 0)),
                  pl.BlockSpec((None, 1, d, 2 * D_FF), wmap),
                  pl.BlockSpec((None, 1, 1, 2 * D_FF), wmap),
                  pl.BlockSpec((None, 1, D_FF, d), wmap),
                  pl.BlockSpec((None, 1, 1, d), wmap)],
        out_specs=pl.BlockSpec((bs, d), lambda i, be, nu: (i, 0)),
    )
    return pl.pallas_call(
        _expert_kernel,
        grid_spec=grid_spec,
        out_shape=jax.ShapeDtypeStruct((n_slots, d), F32),
        compiler_params=_params(vmem, 1),
    )(block_expert, n_used, xs, w1_all, b1_all, w2_all, b2_all)


def _combine_kernel(dest_hbm, y_hbm, gates_ref, x_ref, g_ref, beta_ref, o_ref, dest_smem, buf, idx_sem, row_sem):
    i = pl.program_id(0)
    tb = dest_smem.shape[1]
    idx_cp = pltpu.make_async_copy(dest_hbm.at[i], dest_smem, idx_sem)
    idx_cp.start()
    idx_cp.wait()

    def body(rg, carry):
        for u in range(ISSUE_UNROLL):
            r = rg * ISSUE_UNROLL + u
            for k in range(TOP_K):
                pltpu.make_async_copy(y_hbm.at[pl.ds(dest_smem[k, r], 1)], buf.at[k, pl.ds(r, 1)],
                                      row_sem).start()
        return carry

    lax.fori_loop(0, tb // ISSUE_UNROLL, body, 0)
    for k in range(TOP_K):
        pltpu.make_async_copy(y_hbm.at[pl.ds(0, tb)], buf.at[k], row_sem).wait()

    gates = gates_ref[...]
    f = gates[:, 0:1] * buf[0]
    for k in range(1, TOP_K):
        f = f + gates[:, k:k + 1] * buf[k]
    o_ref[...] = _layer_norm_rows(DEEPNORM_ALPHA * x_ref[...] + f, g_ref[...], beta_ref[...])


def _combine_ln(dest, y, gates_tok, x, g, beta):
    nblk, _, tb = dest.shape
    t, d = x.shape
    any_spec = pl.BlockSpec(memory_space=pl.ANY)
    vec = pl.BlockSpec((1, d), lambda i: (0, 0))
    vmem = _vmem_limit(TOP_K * tb * d * 4, 4 * tb * d * 4, 4 * tb * d * 4)
    return pl.pallas_call(
        _combine_kernel,
        grid=(nblk,),
        in_specs=[any_spec, any_spec,
                  pl.BlockSpec((tb, TOP_K), lambda i: (i, 0)),
                  pl.BlockSpec((tb, d), lambda i: (i, 0)),
                  vec, vec],
        out_specs=pl.BlockSpec((tb, d), lambda i: (i, 0)),
        out_shape=jax.ShapeDtypeStruct((t, d), F32),
        scratch_shapes=[pltpu.SMEM((TOP_K, tb), I32),
                        pltpu.VMEM((TOP_K, tb, d), F32),
                        pltpu.SemaphoreType.DMA(()),
                        pltpu.SemaphoreType.DMA(())],
        compiler_params=_params(vmem, 1),
    )(dest, y, gates_tok, x, g, beta)


def _moe_ln(layer, x, w_router, b_router, w1_all, b1_all, w2_all, b2_all, g, beta):
    t, d = x.shape
    bs = SLOT_BLOCK
    n_blocks = (t * TOP_K) // bs + N_EXPERTS
    n_slots = n_blocks * bs

    e_idx, rank, gates, cnt = _router(x, w_router.T, b_router.reshape(N_EXPERTS, 1))
    counts = cnt[:, 0].astype(I32)
    padded = (counts + bs - 1) // bs * bs
    padded_end = jnp.cumsum(padded)
    start_padded = (padded_end - padded).astype(I32)
    block_start = jnp.arange(n_blocks, dtype=I32) * bs
    block_expert = jnp.minimum(jnp.sum(padded_end[None, :] <= block_start[:, None], axis=1),
                               N_EXPERTS - 1).astype(I32)
    n_used = (padded_end[-1:] // bs).astype(I32)

    first_pad = start_padded + counts
    head = (-first_pad) % SUBLANES
    zero1 = jnp.zeros((1,), I32)
    fill = jnp.stack([jnp.concatenate([first_pad, padded_end[-1:]]),
                      jnp.concatenate([head, zero1]),
                      jnp.concatenate([padded - counts - head, zero1])]).astype(I32)
    dest = _dest_slots(start_padded, e_idx, rank)
    xs = _dispatch(fill, dest, x, n_slots)
    y = _experts(layer, block_expert, n_used, xs, w1_all, b1_all, w2_all, b2_all)
    gates_tok = gates.transpose(0, 2, 1).reshape(t, TOP_K)
    return _combine_ln(dest, y, gates_tok, x, g, beta)


def kernel(x, w_ret_in, w_ret_out, w_swa_in, b_swa_in, w_swa_out, b_swa_out, swa_sinks, ln_mix_g, ln_mix_b,
           w_router, b_router, w_exp_in, b_exp_in, w_exp_out, b_exp_out, ln_ffn_g, ln_ffn_b):
    batch, seq, d = x.shape
    t = batch * seq
    xt = x.reshape(t, d)
    row = lambda v: v.reshape(1, -1)
    zeros_d = jnp.zeros((1, d), F32)
    w1_all = w_exp_in.astype(BF16)
    w2_all = w_exp_out.astype(BF16)
    b1_all = b_exp_in.reshape(DEPTH, N_EXPERTS, 1, 2 * D_FF)
    b2_all = b_exp_out.reshape(DEPTH, N_EXPERTS, 1, d)
    for i in range(DEPTH):
        j = i // 2
        if i % 2 == 0:
            proj = _proj(xt, w_ret_in[j].astype(BF16), jnp.zeros((1, RET_IN), F32), PROJ_N_CHUNK)
            mixed = _retention_core(proj, batch, seq)
            w_out, b_out = w_ret_out[j].astype(BF16), zeros_d
        else:
            proj = _proj(xt, w_swa_in[j].astype(BF16), row(b_swa_in[j]), SWA_IN)
            mixed = _swa_core(proj, swa_sinks[j], batch, seq)
            w_out, b_out = w_swa_out[j].astype(BF16), row(b_swa_out[j])
        xt = _outproj_ln(mixed, w_out, b_out, xt, row(ln_mix_g[i]), row(ln_mix_b[i]))
        xt = _moe_ln(i, xt, w_router[i], b_router[i], w1_all, b1_all, w2_all, b2_all,
                     row(ln_ffn_g[i]), row(ln_ffn_b[i]))
    return xt.reshape(batch, seq, d)
```

```python
import functools
import math

import jax
import jax.numpy as jnp
from jax import lax
from jax.experimental import pallas as pl
from jax.experimental.pallas import tpu as pltpu

F32 = jnp.float32
BF16 = jnp.bfloat16
I32 = jnp.int32
U32 = jnp.uint32

D_MODEL = 1024
DEPTH = 2
RET_HEADS = 4
RET_QK_DIM = 256
RET_V_DIM = 512
RET_CHUNK = 128
RET_QK_ALL = RET_HEADS * RET_QK_DIM
RET_V_ALL = RET_HEADS * RET_V_DIM
RET_IN = 2 * RET_QK_ALL + 2 * RET_V_ALL
SWA_Q_HEADS = 16
SWA_KV_HEADS = 2
SWA_GROUP = 8
SWA_HEAD_DIM = 64
SWA_BLOCK = 128
WINDOW = 128
SWA_Q_ALL = SWA_Q_HEADS * SWA_HEAD_DIM
SWA_KV_ALL = 2 * SWA_KV_HEADS * SWA_HEAD_DIM
SWA_IN = SWA_Q_ALL + SWA_KV_ALL
N_EXPERTS = 32
TOP_K = 4
D_FF = 1024
SWIGLU_ALPHA = 1.702
SWIGLU_LIMIT = 7.0
LN_EPS = 1e-5
GN_EPS = 1e-6
DEEPNORM_ALPHA = (2 * DEPTH) ** 0.25

V7X_VMEM_BYTES = 64 * 1024 * 1024
VMEM_CAP = V7X_VMEM_BYTES - 8 * 1024 * 1024
ROW_TILE = 512
PROJ_N_CHUNK = 1536
TOK_BLOCK = 256
SLOT_BLOCK = 512
SUBLANES = 8
LANES = 128
ROW_TILES = D_MODEL // LANES
assert ROW_TILES == SUBLANES
SORT_ROWS = TOP_K * TOK_BLOCK
SEG_BITS = tuple(1 << b for b in range(TOK_BLOCK.bit_length() - 1, -1, -1))
FILL_BITS = tuple(1 << b for b in range(SLOT_BLOCK.bit_length() - 2, -1, -1))
SEG_TAB = 3 * N_EXPERTS

NT_DIMS = (((1,), (1,)), ((), ()))
TN_DIMS = (((0,), (0,)), ((), ()))


def _vmem_limit(*nbytes):
    est = int(sum(nbytes) * 1.2) + (4 << 20)
    return min(max(est, 16 << 20), VMEM_CAP)


def _params(vmem, n_grid):
    return pltpu.CompilerParams(dimension_semantics=("arbitrary",) * n_grid,
                                vmem_limit_bytes=vmem)


def _layer_norm_rows(y, g, b):
    mu = jnp.mean(y, axis=-1, keepdims=True)
    yc = y - mu
    var = jnp.mean(yc * yc, axis=-1, keepdims=True)
    return yc * lax.rsqrt(var + LN_EPS) * g + b


def _proj_kernel(x_ref, w_ref, b_ref, o_ref, *, n_chunk):
    xb = x_ref[...].astype(BF16)
    n_out = o_ref.shape[1]
    for c in range(0, n_out, n_chunk):
        acc = jnp.dot(xb, w_ref[:, c:c + n_chunk], preferred_element_type=F32)
        o_ref[:, c:c + n_chunk] = (acc + b_ref[:, c:c + n_chunk]).astype(o_ref.dtype)


def _proj(x, w_bf16, bias, n_chunk):
    t, d = x.shape
    n = w_bf16.shape[1]
    tm = ROW_TILE
    vmem = _vmem_limit(2 * tm * d * 4, 2 * d * n * 2, 2 * tm * n * 2, tm * n_chunk * 4 * 2, tm * d * 2)
    return pl.pallas_call(
        functools.partial(_proj_kernel, n_chunk=n_chunk),
        grid=(t // tm,),
        in_specs=[pl.BlockSpec((tm, d), lambda i: (i, 0)),
                  pl.BlockSpec((d, n), lambda i: (0, 0)),
                  pl.BlockSpec((1, n), lambda i: (0, 0))],
        out_specs=pl.BlockSpec((tm, n), lambda i: (i, 0)),
        out_shape=jax.ShapeDtypeStruct((t, n), BF16),
        compiler_params=_params(vmem, 1),
    )(x, w_bf16, bias)


def _ret_gammas():
    return [1.0 - 2.0 ** (-5.0 - h) for h in range(RET_HEADS)]


def _ret_kernel(q_ref, k_ref, v_ref, g_ref, dm_ref, qd_ref, kd_ref, o_ref, state_ref):
    @pl.when(pl.program_id(1) == 0)
    def _():
        state_ref[...] = jnp.zeros_like(state_ref)

    gammas = _ret_gammas()
    for h in range(RET_HEADS):
        qh = q_ref[:, h * RET_QK_DIM:(h + 1) * RET_QK_DIM]
        kh = k_ref[:, h * RET_QK_DIM:(h + 1) * RET_QK_DIM]
        vh = v_ref[:, h * RET_V_DIM:(h + 1) * RET_V_DIM]
        s = lax.dot_general(qh, kh, NT_DIMS, preferred_element_type=F32) * dm_ref[h]
        o = jnp.dot(s.astype(BF16), vh, preferred_element_type=F32)
        st = state_ref[h]
        o = o + jnp.dot(qh, st.astype(BF16), preferred_element_type=F32) * qd_ref[h]
        kdec = (kh.astype(F32) * kd_ref[h]).astype(BF16)
        upd = lax.dot_general(kdec, vh, TN_DIMS, preferred_element_type=F32)
        state_ref[h] = st * (gammas[h] ** RET_CHUNK) + upd
        mu = jnp.mean(o, axis=-1, keepdims=True)
        oc = o - mu
        var = jnp.mean(oc * oc, axis=-1, keepdims=True)
        on = oc * lax.rsqrt(var + GN_EPS)
        gh = g_ref[:, h * RET_V_DIM:(h + 1) * RET_V_DIM].astype(F32)
        gate = gh / (1.0 + jnp.exp(-gh))
        o_ref[:, h * RET_V_DIM:(h + 1) * RET_V_DIM] = (gate * on).astype(o_ref.dtype)


def _retention_core(proj, batch, seq):
    c = RET_CHUNK
    nc = seq // c
    log_g = jnp.log(jnp.asarray(_ret_gammas(), F32))
    pos = jnp.arange(c, dtype=F32)
    rel = pos[:, None] - pos[None, :]
    scale = RET_QK_DIM ** -0.5
    dm = jnp.where(rel >= 0, jnp.exp(log_g[:, None, None] * jnp.maximum(rel, 0.0)), 0.0) * scale
    qd = jnp.exp(log_g[:, None] * (pos + 1.0))[..., None]
    kd = jnp.exp(log_g[:, None] * (c - 1.0 - pos))[..., None] * scale
    t = batch * seq
    row = lambda b, i: b * nc + i
    const3 = lambda b, i: (0, 0, 0)
    vmem = _vmem_limit(2 * c * RET_IN * 2, 2 * c * RET_V_ALL * 2, RET_HEADS * RET_QK_DIM * RET_V_DIM * 4 * 2,
                       8 << 20)
    return pl.pallas_call(
        _ret_kernel,
        grid=(batch, nc),
        in_specs=[pl.BlockSpec((c, RET_QK_ALL), lambda b, i: (row(b, i), 0)),
                  pl.BlockSpec((c, RET_QK_ALL), lambda b, i: (row(b, i), 1)),
                  pl.BlockSpec((c, RET_V_ALL), lambda b, i: (row(b, i), 1)),
                  pl.BlockSpec((c, RET_V_ALL), lambda b, i: (row(b, i), 2)),
                  pl.BlockSpec((RET_HEADS, c, c), const3),
                  pl.BlockSpec((RET_HEADS, c, 1), const3),
                  pl.BlockSpec((RET_HEADS, c, 1), const3)],
        out_specs=pl.BlockSpec((c, RET_V_ALL), lambda b, i: (row(b, i), 0)),
        out_shape=jax.ShapeDtypeStruct((t, RET_V_ALL), BF16),
        scratch_shapes=[pltpu.VMEM((RET_HEADS, RET_QK_DIM, RET_V_DIM), F32)],
        compiler_params=_params(vmem, 2),
    )(proj, proj, proj, proj, dm, qd, kd)


def _swa_kernel(sink_ref, q_ref, kvp_ref, kvc_ref, o_ref):
    n = pl.program_id(1)
    c = SWA_BLOCK
    hd = SWA_HEAD_DIM
    qi = lax.broadcasted_iota(I32, (c, 2 * c), 0)
    kj = lax.broadcasted_iota(I32, (c, 2 * c), 1)
    dist = qi + c - kj
    valid = (dist >= 0) & (dist < WINDOW) & (n * c + kj - c >= 0)
    distf = dist.astype(F32)
    outs = []
    for j in range(SWA_KV_HEADS):
        kcat = jnp.concatenate([kvp_ref[:, j * hd:(j + 1) * hd], kvc_ref[:, j * hd:(j + 1) * hd]], axis=0)
        v0 = SWA_KV_HEADS * hd + j * hd
        vcat = jnp.concatenate([kvp_ref[:, v0:v0 + hd], kvc_ref[:, v0:v0 + hd]], axis=0)
        for g in range(SWA_GROUP):
            h = j * SWA_GROUP + g
            slope = 2.0 ** (-8.0 * (h + 1) / SWA_Q_HEADS)
            qh = q_ref[:, h * hd:(h + 1) * hd]
            s = lax.dot_general(qh, kcat, NT_DIMS, preferred_element_type=F32) * (hd ** -0.5)
            s = jnp.where(valid, s - slope * distf, -jnp.inf)
            sink = sink_ref[h]
            m = jnp.maximum(jnp.max(s, axis=-1, keepdims=True), sink)
            p = jnp.exp(s - m)
            den = jnp.sum(p, axis=-1, keepdims=True) + jnp.exp(sink - m)
            o = jnp.dot(p.astype(BF16), vcat, preferred_element_type=F32) / den
            outs.append(o)
    o_ref[...] = jnp.concatenate(outs, axis=-1).astype(o_ref.dtype)


def _swa_core(proj, sinks, batch, seq):
    c = SWA_BLOCK
    nb = seq // c
    t = batch * seq
    kv_col = SWA_Q_ALL // SWA_KV_ALL
    vmem = _vmem_limit(2 * c * SWA_IN * 2 * 2, 16 << 20)
    return pl.pallas_call(
        _swa_kernel,
        grid=(batch, nb),
        in_specs=[pl.BlockSpec(memory_space=pltpu.SMEM),
                  pl.BlockSpec((c, SWA_Q_ALL), lambda b, n: (b * nb + n, 0)),
                  pl.BlockSpec((c, SWA_KV_ALL), lambda b, n: (b * nb + jnp.maximum(n - 1, 0), kv_col)),
                  pl.BlockSpec((c, SWA_KV_ALL), lambda b, n: (b * nb + n, kv_col))],
        out_specs=pl.BlockSpec((c, SWA_Q_ALL), lambda b, n: (b * nb + n, 0)),
        out_shape=jax.ShapeDtypeStruct((t, SWA_Q_ALL), BF16),
        compiler_params=_params(vmem, 2),
    )(sinks, proj, proj, proj)


def _outproj_ln_kernel(a_ref, w_ref, b_ref, x_ref, g_ref, beta_ref, o_ref):
    m = jnp.dot(a_ref[...], w_ref[...], preferred_element_type=F32) + b_ref[...]
    o_ref[...] = _layer_norm_rows(DEEPNORM_ALPHA * x_ref[...] + m, g_ref[...], beta_ref[...])


def _outproj_ln(a, w_bf16, bias, x, g, beta):
    t, kin = a.shape
    d = D_MODEL
    tm = ROW_TILE
    vec = pl.BlockSpec((1, d), lambda i: (0, 0))
    vmem = _vmem_limit(2 * tm * kin * 2, 2 * kin * d * 2, 4 * tm * d * 4, 4 * tm * d * 4)
    return pl.pallas_call(
        _outproj_ln_kernel,
        grid=(t // tm,),
        in_specs=[pl.BlockSpec((tm, kin), lambda i: (i, 0)),
                  pl.BlockSpec((kin, d), lambda i: (0, 0)),
                  vec,
                  pl.BlockSpec((tm, d), lambda i: (i, 0)),
                  vec, vec],
        out_specs=pl.BlockSpec((tm, d), lambda i: (i, 0)),
        out_shape=jax.ShapeDtypeStruct((t, d), F32),
        compiler_params=_params(vmem, 1),
    )(a, w_bf16, bias, x, g, beta)


def _router_kernel(x_ref, wt_ref, b_ref, pos_ref, gate_ref, nbe_ref, base_ref, cnt_ref, carry_ref):
    @pl.when(pl.program_id(0) == 0)
    def _():
        carry_ref[...] = jnp.zeros_like(carry_ref)

    tb = x_ref.shape[0]
    x = x_ref[...]
    wt = wt_ref[...]
    xh = x.astype(BF16)
    xl = (x - xh.astype(F32)).astype(BF16)
    wh = wt.astype(BF16)
    wl = (wt - wh.astype(F32)).astype(BF16)
    logits = (lax.dot_general(wh, xh, NT_DIMS, preferred_element_type=F32)
              + lax.dot_general(wh, xl, NT_DIMS, preferred_element_type=F32)
              + lax.dot_general(wl, xh, NT_DIMS, preferred_element_type=F32)
              + b_ref[...])

    eidx = lax.broadcasted_iota(I32, (N_EXPERTS, tb), 0).astype(F32)
    work = logits
    sels, vals = [], []
    for _ in range(TOP_K):
        m = jnp.max(work, axis=0, keepdims=True)
        idx = jnp.min(jnp.where(work == m, eidx, float(N_EXPERTS)), axis=0, keepdims=True)
        sel = eidx == idx
        sels.append(sel)
        vals.append(m)
        work = jnp.where(sel, -jnp.inf, work)

    exps = [jnp.exp(v - vals[0]) for v in vals]
    den = exps[0] + exps[1] + exps[2] + exps[3]
    gates = [e / den for e in exps]

    mask = jnp.zeros((N_EXPERTS, tb), F32)
    for sel in sels:
        mask = mask + jnp.where(sel, 1.0, 0.0)
    ti = lax.broadcasted_iota(I32, (tb, tb), 0)
    tj = lax.broadcasted_iota(I32, (tb, tb), 1)
    upper = jnp.where(ti < tj, 1.0, 0.0).astype(BF16)
    rank_loc = jnp.dot(mask.astype(BF16), upper, preferred_element_type=F32)
    n_be = jnp.broadcast_to(jnp.sum(mask, axis=1, keepdims=True), (N_EXPERTS, LANES))
    ei = lax.broadcasted_iota(I32, (N_EXPERTS, N_EXPERTS), 0)
    ej = lax.broadcasted_iota(I32, (N_EXPERTS, N_EXPERTS), 1)
    lower = jnp.where(ej < ei, 1.0, 0.0).astype(BF16)
    off_be = jnp.dot(lower, n_be.astype(BF16), preferred_element_type=F32)
    pos_all = rank_loc + off_be[:, :1]
    poss = [jnp.sum(jnp.where(sel, pos_all, 0.0), axis=0, keepdims=True) for sel in sels]

    pos_ref[0] = jnp.concatenate(poss, axis=0).astype(I32)
    gate_ref[0] = jnp.concatenate(gates, axis=0)
    carry = carry_ref[...]
    nbe_ref[0] = n_be
    base_ref[0] = carry
    new_carry = carry + n_be
    carry_ref[...] = new_carry
    cnt_ref[...] = new_carry


def _router(x, w_router_t, b_router_col):
    t, d = x.shape
    tb = TOK_BLOCK
    assert tb <= 256, "per-block expert counts must stay exact in bf16"
    nblk = t // tb
    blk3 = pl.BlockSpec((1, TOP_K, tb), lambda i: (i, 0, 0))
    seg3 = pl.BlockSpec((1, N_EXPERTS, LANES), lambda i: (i, 0, 0))
    vmem = _vmem_limit(2 * tb * d * 4, 2 * N_EXPERTS * d * 4, 8 << 20)
    return pl.pallas_call(
        _router_kernel,
        grid=(nblk,),
        in_specs=[pl.BlockSpec((tb, d), lambda i: (i, 0)),
                  pl.BlockSpec((N_EXPERTS, d), lambda i: (0, 0)),
                  pl.BlockSpec((N_EXPERTS, 1), lambda i: (0, 0))],
        out_specs=[blk3, blk3, seg3, seg3, pl.BlockSpec((N_EXPERTS, LANES), lambda i: (0, 0))],
        out_shape=[jax.ShapeDtypeStruct((nblk, TOP_K, tb), I32),
                   jax.ShapeDtypeStruct((nblk, TOP_K, tb), F32),
                   jax.ShapeDtypeStruct((nblk, N_EXPERTS, LANES), F32),
                   jax.ShapeDtypeStruct((nblk, N_EXPERTS, LANES), F32),
                   jax.ShapeDtypeStruct((N_EXPERTS, LANES), F32)],
        scratch_shapes=[pltpu.VMEM((N_EXPERTS, LANES), F32)],
        compiler_params=_params(vmem, 1),
    )(x, w_router_t, b_router_col)


def _rows(ref, row, n):
    return ref.at[pl.ds(pl.multiple_of(row * ROW_TILES, ROW_TILES), n * ROW_TILES)]


def _to_row_tiled(ref, value):
    n = value.shape[0]
    for c in range(ROW_TILES):
        ref[pl.ds(c, n, stride=ROW_TILES), :] = value[:, c * LANES:(c + 1) * LANES]


def _from_row_tiled(ref, n):
    return jnp.concatenate([ref[pl.ds(c, n, stride=ROW_TILES), :] for c in range(ROW_TILES)], axis=1)


def _run_copies(n, src_ref, src_row, dst_ref, dst_row, sem, bits, act, advance_src=True):
    for bit in bits:
        @pl.when((n & bit) != 0)
        def _():
            act(pltpu.make_async_copy(_rows(src_ref, src_row, bit), _rows(dst_ref, dst_row, bit), sem))
        if advance_src:
            src_row = src_row + (n & bit)
        dst_row = dst_row + (n & bit)


def _start(cp):
    cp.start()


def _wait(cp):
    cp.wait()


def _segment_loop(tab_ref, blk, act_on_run):
    base = blk * SEG_TAB

    def body(e, carry):
        act_on_run(tab_ref[base + e], tab_ref[base + N_EXPERTS + e], tab_ref[base + 2 * N_EXPERTS + e])
        return carry

    lax.fori_loop(0, N_EXPERTS, body, 0)


def _dispatch_kernel(tab_ref, fill_ref, pos_ref, x_ref, xs_hbm, sbuf, zeros_ref, sems, fill_sem, *, nblk):
    i = pl.program_id(0)
    par = i % 2
    tb = x_ref.shape[0]
    zrows = zeros_ref.shape[0] // ROW_TILES

    @pl.when(i == 0)
    def _():
        zeros_ref[...] = jnp.zeros_like(zeros_ref)

        def fill(act):
            def body(e, carry):
                _run_copies(fill_ref[1, e], zeros_ref, 0, xs_hbm, fill_ref[0, e], fill_sem, FILL_BITS, act,
                            advance_src=False)
                return carry
            lax.fori_loop(0, N_EXPERTS, body, 0)

        fill(_start)
        fill(_wait)

        first_tail = fill_ref[0, N_EXPERTS] // zrows
        n_chunks = xs_hbm.shape[0] // (zrows * ROW_TILES)

        def tail(act):
            def body(j, carry):
                act(pltpu.make_async_copy(zeros_ref, _rows(xs_hbm, j * zrows, zrows), fill_sem))
                return carry
            lax.fori_loop(first_tail, n_chunks, body, 0)

        tail(_start)
        tail(_wait)

    def drain(slot):
        pltpu.make_async_copy(sbuf.at[slot], _rows(xs_hbm, 0, SORT_ROWS), sems.at[slot]).wait()

    @pl.when(i >= 2)
    def _():
        drain(par)

    pos = pos_ref[0]
    pi = lax.broadcasted_iota(I32, (SORT_ROWS, tb), 0)
    onehot = jnp.where(pi == pos[0:1, :], 1.0, 0.0)
    for k in range(1, TOP_K):
        onehot = onehot + jnp.where(pi == pos[k:k + 1, :], 1.0, 0.0)
    xsorted = jnp.dot(onehot.astype(BF16), x_ref[...].astype(BF16), preferred_element_type=F32)
    buf = sbuf.at[par]
    _to_row_tiled(buf, xsorted)

    _segment_loop(tab_ref, i, lambda n, off, slot: _run_copies(n, buf, off, xs_hbm, slot, sems.at[par],
                                                               SEG_BITS, _start))

    @pl.when(i == nblk - 1)
    def _():
        drain(par)
        if nblk >= 2:
            drain(1 - par)


def _dispatch(tab, fill, pos, x, n_slots):
    nblk, _, tb = pos.shape
    d = x.shape[1]
    zrows = SLOT_BLOCK // 2
    vmem = _vmem_limit(2 * tb * d * 4, 2 * SORT_ROWS * d * 4, zrows * d * 4, 3 * SORT_ROWS * d * 4)
    grid_spec = pltpu.PrefetchScalarGridSpec(
        num_scalar_prefetch=2,
        grid=(nblk,),
        in_specs=[pl.BlockSpec((1, TOP_K, tb), lambda i, tab, fill: (i, 0, 0)),
                  pl.BlockSpec((tb, d), lambda i, tab, fill: (i, 0))],
        out_specs=pl.BlockSpec(memory_space=pl.ANY),
        scratch_shapes=[pltpu.VMEM((2, SORT_ROWS * ROW_TILES, LANES), F32),
                        pltpu.VMEM((zrows * ROW_TILES, LANES), F32),
                        pltpu.SemaphoreType.DMA((2,)),
                        pltpu.SemaphoreType.DMA(())],
    )
    return pl.pallas_call(
        functools.partial(_dispatch_kernel, nblk=nblk),
        grid_spec=grid_spec,
        out_shape=jax.ShapeDtypeStruct((n_slots * ROW_TILES, LANES), F32),
        compiler_params=pltpu.CompilerParams(dimension_semantics=("arbitrary",), has_side_effects=True,
                                             vmem_limit_bytes=vmem),
    )(tab, fill, pos, x)


def _expert_kernel(be_ref, nu_ref, xs_ref, w1_ref, b1_ref, w2_ref, b2_ref, y_ref):
    i = pl.program_id(0)

    @pl.when(i < nu_ref[0])
    def _():
        xb = _from_row_tiled(xs_ref, SLOT_BLOCK).astype(BF16)
        h = jnp.dot(xb, w1_ref[0], preferred_element_type=F32) + b1_ref[0]
        glu = jnp.minimum(h[:, :D_FF], SWIGLU_LIMIT)
        lin = jnp.clip(h[:, D_FF:], -SWIGLU_LIMIT, SWIGLU_LIMIT)
        act = glu / (1.0 + jnp.exp(-SWIGLU_ALPHA * glu)) * (lin + 1.0)
        _to_row_tiled(y_ref, jnp.dot(act.astype(BF16), w2_ref[0], preferred_element_type=F32) + b2_ref[0])

    @pl.when(i >= nu_ref[0])
    def _():
        y_ref[...] = jnp.zeros_like(y_ref)


def _experts(layer, block_expert, n_used, xs, w1_all, b1_all, w2_all, b2_all):
    n_slots = xs.shape[0] // ROW_TILES
    bs = SLOT_BLOCK
    d = D_MODEL
    vmem = _vmem_limit(2 * bs * d * 4, 2 * d * 2 * D_FF * 2, 2 * D_FF * d * 2, 2 * bs * d * 4,
                       3 * bs * 2 * D_FF * 4)
    wmap = lambda i, be, nu: (layer, be[i], 0, 0)
    slot_blk = pl.BlockSpec((bs * ROW_TILES, LANES), lambda i, be, nu: (i, 0))
    grid_spec = pltpu.PrefetchScalarGridSpec(
        num_scalar_prefetch=2,
        grid=(n_slots // bs,),
        in_specs=[slot_blk,
                  pl.BlockSpec((None, 1, d, 2 * D_FF), wmap),
                  pl.BlockSpec((None, 1, 1, 2 * D_FF), wmap),
                  pl.BlockSpec((None, 1, D_FF, d), wmap),
                  pl.BlockSpec((None, 1, 1, d), wmap)],
        out_specs=slot_blk,
    )
    return pl.pallas_call(
        _expert_kernel,
        grid_spec=grid_spec,
        out_shape=jax.ShapeDtypeStruct((n_slots * ROW_TILES, LANES), F32),
        compiler_params=_params(vmem, 1),
    )(block_expert, n_used, xs, w1_all, b1_all, w2_all, b2_all)


def _combine_kernel(tab_ref, y_hbm, pos_ref, gates_ref, x_ref, g_ref, beta_ref, o_ref, ybuf, sems, *, nblk):
    i = pl.program_id(0)
    par = i % 2
    tb = x_ref.shape[0]

    def fetch(blk, slot):
        dst = ybuf.at[slot]
        _segment_loop(tab_ref, blk, lambda n, off, first: _run_copies(n, y_hbm, first, dst, off, sems.at[slot],
                                                                      SEG_BITS, _start))

    @pl.when(i == 0)
    def _():
        fetch(0, 0)

    @pl.when(i + 1 < nblk)
    def _():
        fetch(i + 1, 1 - par)

    pltpu.make_async_copy(_rows(y_hbm, 0, SORT_ROWS), ybuf.at[par], sems.at[par]).wait()
    ysorted = _from_row_tiled(ybuf.at[par], SORT_ROWS)

    pos = pos_ref[...]
    gates = gates_ref[...]
    pj = lax.broadcasted_iota(I32, (tb, SORT_ROWS), 1)
    wsel = jnp.where(pj == pos[:, 0:1], gates[:, 0:1], 0.0)
    for k in range(1, TOP_K):
        wsel = wsel + jnp.where(pj == pos[:, k:k + 1], gates[:, k:k + 1], 0.0)
    w_hi = wsel.astype(BF16)
    w_lo = (wsel - w_hi.astype(F32)).astype(BF16)
    y_hi = ysorted.astype(BF16)
    y_lo = (ysorted - y_hi.astype(F32)).astype(BF16)
    f = (jnp.dot(w_hi, y_hi, preferred_element_type=F32)
         + jnp.dot(w_lo, y_hi, preferred_element_type=F32)
         + jnp.dot(w_hi, y_lo, preferred_element_type=F32))
    o_ref[...] = _layer_norm_rows(DEEPNORM_ALPHA * x_ref[...] + f, g_ref[...], beta_ref[...])


def _combine_ln(tab, y, pos_tok, gates_tok, x, g, beta):
    t, d = x.shape
    tb = TOK_BLOCK
    nblk = t // tb
    vec = pl.BlockSpec((1, d), lambda i, tab: (0, 0))
    tok4 = pl.BlockSpec((tb, TOP_K), lambda i, tab: (i, 0))
    vmem = _vmem_limit(2 * SORT_ROWS * d * 4, 4 * tb * d * 4, 3 * SORT_ROWS * d * 4, 2 * tb * SORT_ROWS * 4)
    grid_spec = pltpu.PrefetchScalarGridSpec(
        num_scalar_prefetch=1,
        grid=(nblk,),
        in_specs=[pl.BlockSpec(memory_space=pl.ANY), tok4, tok4,
                  pl.BlockSpec((tb, d), lambda i, tab: (i, 0)),
                  vec, vec],
        out_specs=pl.BlockSpec((tb, d), lambda i, tab: (i, 0)),
        scratch_shapes=[pltpu.VMEM((2, SORT_ROWS * ROW_TILES, LANES), F32),
                        pltpu.SemaphoreType.DMA((2,))],
    )
    return pl.pallas_call(
        functools.partial(_combine_kernel, nblk=nblk),
        grid_spec=grid_spec,
        out_shape=jax.ShapeDtypeStruct((t, d), F32),
        compiler_params=_params(vmem, 1),
    )(tab, y, pos_tok, gates_tok, x, g, beta)


def _moe_ln(layer, x, w_router, b_router, w1_all, b1_all, w2_all, b2_all, g, beta):
    t, d = x.shape
    bs = SLOT_BLOCK
    n_blocks = (t * TOP_K) // bs + N_EXPERTS
    n_slots = n_blocks * bs

    pos, gates, nbe, base, cnt = _router(x, w_router.T, b_router.reshape(N_EXPERTS, 1))
    counts = cnt[:, 0].astype(I32)
    padded = (counts + bs - 1) // bs * bs
    padded_end = jnp.cumsum(padded)
    start_padded = (padded_end - padded).astype(I32)
    block_start = jnp.arange(n_blocks, dtype=I32) * bs
    block_expert = jnp.minimum(jnp.sum(padded_end[None, :] <= block_start[:, None], axis=1),
                               N_EXPERTS - 1).astype(I32)
    n_used = (padded_end[-1:] // bs).astype(I32)

    run_len = nbe[:, :, 0].astype(I32)
    run_off = jnp.cumsum(run_len, axis=1) - run_len
    run_slot = start_padded[None, :] + base[:, :, 0].astype(I32)
    tab = jnp.concatenate([run_len, run_off, run_slot], axis=1).reshape(-1)
    zero1 = jnp.zeros((1,), I32)
    fill = jnp.stack([jnp.concatenate([start_padded + counts, padded_end[-1:]]),
                      jnp.concatenate([padded - counts, zero1])]).astype(I32)

    xs = _dispatch(tab, fill, pos, x, n_slots)
    y = _experts(layer, block_expert, n_used, xs, w1_all, b1_all, w2_all, b2_all)
    pos_tok = pos.transpose(0, 2, 1).reshape(t, TOP_K)
    gates_tok = gates.transpose(0, 2, 1).reshape(t, TOP_K)
    return _combine_ln(tab, y, pos_tok, gates_tok, x, g, beta)


def kernel(x, w_ret_in, w_ret_out, w_swa_in, b_swa_in, w_swa_out, b_swa_out, swa_sinks, ln_mix_g, ln_mix_b,
           w_router, b_router, w_exp_in, b_exp_in, w_exp_out, b_exp_out, ln_ffn_g, ln_ffn_b):
    batch, seq, d = x.shape
    t = batch * seq
    xt = x.reshape(t, d)
    row = lambda v: v.reshape(1, -1)
    zeros_d = jnp.zeros((1, d), F32)
    w1_all = w_exp_in.astype(BF16)
    w2_all = w_exp_out.astype(BF16)
    b1_all = b_exp_in.reshape(DEPTH, N_EXPERTS, 1, 2 * D_FF)
    b2_all = b_exp_out.reshape(DEPTH, N_EXPERTS, 1, d)
    for i in range(DEPTH):
        j = i // 2
        if i % 2 == 0:
            proj = _proj(xt, w_ret_in[j].astype(BF16), jnp.zeros((1, RET_IN), F32), PROJ_N_CHUNK)
            mixed = _retention_core(proj, batch, seq)
            w_out, b_out = w_ret_out[j].astype(BF16), zeros_d
        else:
            proj = _proj(xt, w_swa_in[j].astype(BF16), row(b_swa_in[j]), SWA_IN)
            mixed = _swa_core(proj, swa_sinks[j], batch, seq)
            w_out, b_out = w_swa_out[j].astype(BF16), row(b_swa_out[j])
        xt = _outproj_ln(mixed, w_out, b_out, xt, row(ln_mix_g[i]), row(ln_mix_b[i]))
        xt = _moe_ln(i, xt, w_router[i], b_router[i], w1_all, b1_all, w2_all, b2_all,
                     row(ln_ffn_g[i]), row(ln_ffn_b[i]))
    return xt.reshape(batch, seq, d)
```

```python
import functools
import math

import jax
import jax.numpy as jnp
from jax import lax
from jax.experimental import pallas as pl
from jax.experimental.pallas import tpu as pltpu

F32 = jnp.float32
BF16 = jnp.bfloat16
I32 = jnp.int32
U32 = jnp.uint32

D_MODEL = 1024
DEPTH = 2
RET_HEADS = 4
RET_QK_DIM = 256
RET_V_DIM = 512
RET_CHUNK = 128
RET_QK_ALL = RET_HEADS * RET_QK_DIM
RET_V_ALL = RET_HEADS * RET_V_DIM
RET_IN = 2 * RET_QK_ALL + 2 * RET_V_ALL
SWA_Q_HEADS = 16
SWA_KV_HEADS = 2
SWA_GROUP = 8
SWA_HEAD_DIM = 64
SWA_BLOCK = 128
WINDOW = 128
SWA_Q_ALL = SWA_Q_HEADS * SWA_HEAD_DIM
SWA_KV_ALL = 2 * SWA_KV_HEADS * SWA_HEAD_DIM
SWA_IN = SWA_Q_ALL + SWA_KV_ALL
N_EXPERTS = 32
TOP_K = 4
D_FF = 1024
SWIGLU_ALPHA = 1.702
SWIGLU_LIMIT = 7.0
LN_EPS = 1e-5
GN_EPS = 1e-6
DEEPNORM_ALPHA = (2 * DEPTH) ** 0.25

V7X_VMEM_BYTES = 64 * 1024 * 1024
VMEM_CAP = V7X_VMEM_BYTES - 8 * 1024 * 1024
ROW_TILE = 512
RET_KERNEL_CHUNK = 256
PROJ_N_CHUNK = 1536
TOK_BLOCK = 256
SLOT_BLOCK = 512
SUBLANES = 8
LANES = 128
ROW_TILES = D_MODEL // LANES
assert ROW_TILES == SUBLANES
SORT_ROWS = TOP_K * TOK_BLOCK
SEG_BITS = tuple(1 << b for b in range(TOK_BLOCK.bit_length() - 1, -1, -1))
FILL_BITS = tuple(1 << b for b in range(SLOT_BLOCK.bit_length() - 2, -1, -1))
SEG_TAB = 3 * N_EXPERTS

NT_DIMS = (((1,), (1,)), ((), ()))
TN_DIMS = (((0,), (0,)), ((), ()))


def _vmem_limit(*nbytes):
    est = int(sum(nbytes) * 1.2) + (4 << 20)
    return min(max(est, 16 << 20), VMEM_CAP)


def _params(vmem, n_grid):
    return pltpu.CompilerParams(dimension_semantics=("arbitrary",) * n_grid,
                                vmem_limit_bytes=vmem)


def _layer_norm_rows(y, g, b):
    mu = jnp.mean(y, axis=-1, keepdims=True)
    yc = y - mu
    var = jnp.mean(yc * yc, axis=-1, keepdims=True)
    return yc * lax.rsqrt(var + LN_EPS) * g + b


def _proj_kernel(x_ref, w_ref, b_ref, o_ref, *, n_chunk):
    xb = x_ref[...].astype(BF16)
    n_out = o_ref.shape[1]
    for c in range(0, n_out, n_chunk):
        acc = jnp.dot(xb, w_ref[:, c:c + n_chunk], preferred_element_type=F32)
        o_ref[:, c:c + n_chunk] = (acc + b_ref[:, c:c + n_chunk]).astype(o_ref.dtype)


def _proj(x, w_bf16, bias, n_chunk):
    t, d = x.shape
    n = w_bf16.shape[1]
    tm = ROW_TILE
    vmem = _vmem_limit(2 * tm * d * 4, 2 * d * n * 2, 2 * tm * n * 2, tm * n_chunk * 4 * 2, tm * d * 2)
    return pl.pallas_call(
        functools.partial(_proj_kernel, n_chunk=n_chunk),
        grid=(t // tm,),
        in_specs=[pl.BlockSpec((tm, d), lambda i: (i, 0)),
                  pl.BlockSpec((d, n), lambda i: (0, 0)),
                  pl.BlockSpec((1, n), lambda i: (0, 0))],
        out_specs=pl.BlockSpec((tm, n), lambda i: (i, 0)),
        out_shape=jax.ShapeDtypeStruct((t, n), BF16),
        compiler_params=_params(vmem, 1),
    )(x, w_bf16, bias)


def _ret_gammas():
    return [1.0 - 2.0 ** (-5.0 - h) for h in range(RET_HEADS)]


def _ret_kernel(q_ref, k_ref, v_ref, g_ref, dm_ref, qd_ref, kd_ref, o_ref, state_ref):
    @pl.when(pl.program_id(1) == 0)
    def _():
        state_ref[...] = jnp.zeros_like(state_ref)

    gammas = _ret_gammas()
    for h in range(RET_HEADS):
        qh = q_ref[:, h * RET_QK_DIM:(h + 1) * RET_QK_DIM]
        kh = k_ref[:, h * RET_QK_DIM:(h + 1) * RET_QK_DIM]
        vh = v_ref[:, h * RET_V_DIM:(h + 1) * RET_V_DIM]
        s = lax.dot_general(qh, kh, NT_DIMS, preferred_element_type=F32) * dm_ref[h]
        o = jnp.dot(s.astype(BF16), vh, preferred_element_type=F32)
        st = state_ref[h]
        o = o + jnp.dot(qh, st.astype(BF16), preferred_element_type=F32) * qd_ref[h]
        kdec = (kh.astype(F32) * kd_ref[h]).astype(BF16)
        upd = lax.dot_general(kdec, vh, TN_DIMS, preferred_element_type=F32)
        state_ref[h] = st * (gammas[h] ** q_ref.shape[0]) + upd
        mu = jnp.mean(o, axis=-1, keepdims=True)
        oc = o - mu
        var = jnp.mean(oc * oc, axis=-1, keepdims=True)
        on = oc * lax.rsqrt(var + GN_EPS)
        gh = g_ref[:, h * RET_V_DIM:(h + 1) * RET_V_DIM].astype(F32)
        gate = gh / (1.0 + jnp.exp(-gh))
        o_ref[:, h * RET_V_DIM:(h + 1) * RET_V_DIM] = (gate * on).astype(o_ref.dtype)


def _retention_core(proj, batch, seq):
    c = RET_KERNEL_CHUNK
    nc = seq // c
    log_g = jnp.log(jnp.asarray(_ret_gammas(), F32))
    pos = jnp.arange(c, dtype=F32)
    rel = pos[:, None] - pos[None, :]
    scale = RET_QK_DIM ** -0.5
    dm = jnp.where(rel >= 0, jnp.exp(log_g[:, None, None] * jnp.maximum(rel, 0.0)), 0.0) * scale
    qd = jnp.exp(log_g[:, None] * (pos + 1.0))[..., None]
    kd = jnp.exp(log_g[:, None] * (c - 1.0 - pos))[..., None] * scale
    t = batch * seq
    row = lambda b, i: b * nc + i
    const3 = lambda b, i: (0, 0, 0)
    vmem = _vmem_limit(2 * c * RET_IN * 2, 2 * c * RET_V_ALL * 2, RET_HEADS * RET_QK_DIM * RET_V_DIM * 4 * 2,
                       8 << 20)
    return pl.pallas_call(
        _ret_kernel,
        grid=(batch, nc),
        in_specs=[pl.BlockSpec((c, RET_QK_ALL), lambda b, i: (row(b, i), 0)),
                  pl.BlockSpec((c, RET_QK_ALL), lambda b, i: (row(b, i), 1)),
                  pl.BlockSpec((c, RET_V_ALL), lambda b, i: (row(b, i), 1)),
                  pl.BlockSpec((c, RET_V_ALL), lambda b, i: (row(b, i), 2)),
                  pl.BlockSpec((RET_HEADS, c, c), const3),
                  pl.BlockSpec((RET_HEADS, c, 1), const3),
                  pl.BlockSpec((RET_HEADS, c, 1), const3)],
        out_specs=pl.BlockSpec((c, RET_V_ALL), lambda b, i: (row(b, i), 0)),
        out_shape=jax.ShapeDtypeStruct((t, RET_V_ALL), BF16),
        scratch_shapes=[pltpu.VMEM((RET_HEADS, RET_QK_DIM, RET_V_DIM), F32)],
        compiler_params=_params(vmem, 2),
    )(proj, proj, proj, proj, dm, qd, kd)


def _swa_kernel(sink_ref, q_ref, kvp_ref, kvc_ref, o_ref):
    n = pl.program_id(1)
    c = SWA_BLOCK
    hd = SWA_HEAD_DIM
    qi = lax.broadcasted_iota(I32, (c, 2 * c), 0)
    kj = lax.broadcasted_iota(I32, (c, 2 * c), 1)
    dist = qi + c - kj
    valid = (dist >= 0) & (dist < WINDOW) & (n * c + kj - c >= 0)
    distf = dist.astype(F32)
    outs = []
    for j in range(SWA_KV_HEADS):
        kcat = jnp.concatenate([kvp_ref[:, j * hd:(j + 1) * hd], kvc_ref[:, j * hd:(j + 1) * hd]], axis=0)
        v0 = SWA_KV_HEADS * hd + j * hd
        vcat = jnp.concatenate([kvp_ref[:, v0:v0 + hd], kvc_ref[:, v0:v0 + hd]], axis=0)
        for g in range(SWA_GROUP):
            h = j * SWA_GROUP + g
            slope = 2.0 ** (-8.0 * (h + 1) / SWA_Q_HEADS)
            qh = q_ref[:, h * hd:(h + 1) * hd]
            s = lax.dot_general(qh, kcat, NT_DIMS, preferred_element_type=F32) * (hd ** -0.5)
            s = jnp.where(valid, s - slope * distf, -jnp.inf)
            sink = sink_ref[h]
            m = jnp.maximum(jnp.max(s, axis=-1, keepdims=True), sink)
            p = jnp.exp(s - m)
            den = jnp.sum(p, axis=-1, keepdims=True) + jnp.exp(sink - m)
            o = jnp.dot(p.astype(BF16), vcat, preferred_element_type=F32) / den
            outs.append(o)
    o_ref[...] = jnp.concatenate(outs, axis=-1).astype(o_ref.dtype)


def _swa_core(proj, sinks, batch, seq):
    c = SWA_BLOCK
    nb = seq // c
    t = batch * seq
    kv_col = SWA_Q_ALL // SWA_KV_ALL
    vmem = _vmem_limit(2 * c * SWA_IN * 2 * 2, 16 << 20)
    return pl.pallas_call(
        _swa_kernel,
        grid=(batch, nb),
        in_specs=[pl.BlockSpec(memory_space=pltpu.SMEM),
                  pl.BlockSpec((c, SWA_Q_ALL), lambda b, n: (b * nb + n, 0)),
                  pl.BlockSpec((c, SWA_KV_ALL), lambda b, n: (b * nb + jnp.maximum(n - 1, 0), kv_col)),
                  pl.BlockSpec((c, SWA_KV_ALL), lambda b, n: (b * nb + n, kv_col))],
        out_specs=pl.BlockSpec((c, SWA_Q_ALL), lambda b, n: (b * nb + n, 0)),
        out_shape=jax.ShapeDtypeStruct((t, SWA_Q_ALL), BF16),
        compiler_params=_params(vmem, 2),
    )(sinks, proj, proj, proj)


def _outproj_ln_kernel(a_ref, w_ref, b_ref, x_ref, g_ref, beta_ref, o_ref):
    m = jnp.dot(a_ref[...], w_ref[...], preferred_element_type=F32) + b_ref[...]
    o_ref[...] = _layer_norm_rows(DEEPNORM_ALPHA * x_ref[...] + m, g_ref[...], beta_ref[...])


def _outproj_ln(a, w_bf16, bias, x, g, beta):
    t, kin = a.shape
    d = D_MODEL
    tm = ROW_TILE
    vec = pl.BlockSpec((1, d), lambda i: (0, 0))
    vmem = _vmem_limit(2 * tm * kin * 2, 2 * kin * d * 2, 4 * tm * d * 4, 4 * tm * d * 4)
    return pl.pallas_call(
        _outproj_ln_kernel,
        grid=(t // tm,),
        in_specs=[pl.BlockSpec((tm, kin), lambda i: (i, 0)),
                  pl.BlockSpec((kin, d), lambda i: (0, 0)),
                  vec,
                  pl.BlockSpec((tm, d), lambda i: (i, 0)),
                  vec, vec],
        out_specs=pl.BlockSpec((tm, d), lambda i: (i, 0)),
        out_shape=jax.ShapeDtypeStruct((t, d), F32),
        compiler_params=_params(vmem, 1),
    )(a, w_bf16, bias, x, g, beta)


def _router_kernel(x_ref, wt_ref, b_ref, pos_ref, gate_ref, nbe_ref, base_ref, cnt_ref, carry_ref):
    @pl.when(pl.program_id(0) == 0)
    def _():
        carry_ref[...] = jnp.zeros_like(carry_ref)

    tb = x_ref.shape[0]
    x = x_ref[...]
    wt = wt_ref[...]
    xh = x.astype(BF16)
    xl = (x - xh.astype(F32)).astype(BF16)
    wh = wt.astype(BF16)
    wl = (wt - wh.astype(F32)).astype(BF16)
    logits = (lax.dot_general(wh, xh, NT_DIMS, preferred_element_type=F32)
              + lax.dot_general(wh, xl, NT_DIMS, preferred_element_type=F32)
              + lax.dot_general(wl, xh, NT_DIMS, preferred_element_type=F32)
              + b_ref[...])

    eidx = lax.broadcasted_iota(I32, (N_EXPERTS, tb), 0).astype(F32)
    work = logits
    sels, vals = [], []
    for _ in range(TOP_K):
        m = jnp.max(work, axis=0, keepdims=True)
        idx = jnp.min(jnp.where(work == m, eidx, float(N_EXPERTS)), axis=0, keepdims=True)
        sel = eidx == idx
        sels.append(sel)
        vals.append(m)
        work = jnp.where(sel, -jnp.inf, work)

    exps = [jnp.exp(v - vals[0]) for v in vals]
    den = exps[0] + exps[1] + exps[2] + exps[3]
    gates = [e / den for e in exps]

    mask = jnp.zeros((N_EXPERTS, tb), F32)
    for sel in sels:
        mask = mask + jnp.where(sel, 1.0, 0.0)
    ti = lax.broadcasted_iota(I32, (tb, tb), 0)
    tj = lax.broadcasted_iota(I32, (tb, tb), 1)
    upper = jnp.where(ti < tj, 1.0, 0.0).astype(BF16)
    rank_loc = jnp.dot(mask.astype(BF16), upper, preferred_element_type=F32)
    n_be = jnp.broadcast_to(jnp.sum(mask, axis=1, keepdims=True), (N_EXPERTS, LANES))
    ei = lax.broadcasted_iota(I32, (N_EXPERTS, N_EXPERTS), 0)
    ej = lax.broadcasted_iota(I32, (N_EXPERTS, N_EXPERTS), 1)
    lower = jnp.where(ej < ei, 1.0, 0.0).astype(BF16)
    off_be = jnp.dot(lower, n_be.astype(BF16), preferred_element_type=F32)
    pos_all = rank_loc + off_be[:, :1]
    poss = [jnp.sum(jnp.where(sel, pos_all, 0.0), axis=0, keepdims=True) for sel in sels]

    pos_ref[0] = jnp.concatenate(poss, axis=0).astype(I32)
    gate_ref[0] = jnp.concatenate(gates, axis=0)
    carry = carry_ref[...]
    nbe_ref[0] = n_be
    base_ref[0] = carry
    new_carry = carry + n_be
    carry_ref[...] = new_carry
    cnt_ref[...] = new_carry


def _router(x, w_router_t, b_router_col):
    t, d = x.shape
    tb = TOK_BLOCK
    assert tb <= 256, "per-block expert counts must stay exact in bf16"
    nblk = t // tb
    blk3 = pl.BlockSpec((1, TOP_K, tb), lambda i: (i, 0, 0))
    seg3 = pl.BlockSpec((1, N_EXPERTS, LANES), lambda i: (i, 0, 0))
    vmem = _vmem_limit(2 * tb * d * 4, 2 * N_EXPERTS * d * 4, 8 << 20)
    return pl.pallas_call(
        _router_kernel,
        grid=(nblk,),
        in_specs=[pl.BlockSpec((tb, d), lambda i: (i, 0)),
                  pl.BlockSpec((N_EXPERTS, d), lambda i: (0, 0)),
                  pl.BlockSpec((N_EXPERTS, 1), lambda i: (0, 0))],
        out_specs=[blk3, blk3, seg3, seg3, pl.BlockSpec((N_EXPERTS, LANES), lambda i: (0, 0))],
        out_shape=[jax.ShapeDtypeStruct((nblk, TOP_K, tb), I32),
                   jax.ShapeDtypeStruct((nblk, TOP_K, tb), F32),
                   jax.ShapeDtypeStruct((nblk, N_EXPERTS, LANES), F32),
                   jax.ShapeDtypeStruct((nblk, N_EXPERTS, LANES), F32),
                   jax.ShapeDtypeStruct((N_EXPERTS, LANES), F32)],
        scratch_shapes=[pltpu.VMEM((N_EXPERTS, LANES), F32)],
        compiler_params=_params(vmem, 1),
    )(x, w_router_t, b_router_col)


def _rows(ref, row, n):
    return ref.at[pl.ds(pl.multiple_of(row * ROW_TILES, ROW_TILES), n * ROW_TILES)]


def _to_row_tiled(ref, value):
    n = value.shape[0]
    for c in range(ROW_TILES):
        ref[pl.ds(c, n, stride=ROW_TILES), :] = value[:, c * LANES:(c + 1) * LANES]


def _from_row_tiled(ref, n):
    return jnp.concatenate([ref[pl.ds(c, n, stride=ROW_TILES), :] for c in range(ROW_TILES)], axis=1)


def _run_copies(n, src_ref, src_row, dst_ref, dst_row, sem, bits, act, advance_src=True):
    for bit in bits:
        @pl.when((n & bit) != 0)
        def _():
            act(pltpu.make_async_copy(_rows(src_ref, src_row, bit), _rows(dst_ref, dst_row, bit), sem))
        if advance_src:
            src_row = src_row + (n & bit)
        dst_row = dst_row + (n & bit)


def _start(cp):
    cp.start()


def _wait(cp):
    cp.wait()


def _segment_loop(tab_ref, blk, act_on_run):
    base = blk * SEG_TAB

    def body(e, carry):
        act_on_run(tab_ref[base + e], tab_ref[base + N_EXPERTS + e], tab_ref[base + 2 * N_EXPERTS + e])
        return carry

    lax.fori_loop(0, N_EXPERTS, body, 0)


def _dispatch_kernel(tab_ref, fill_ref, pos_ref, x_ref, xs_hbm, sbuf, zeros_ref, sems, fill_sem, *, nblk):
    i = pl.program_id(0)
    par = i % 2
    tb = x_ref.shape[0]
    zrows = zeros_ref.shape[0] // ROW_TILES

    @pl.when(i == 0)
    def _():
        zeros_ref[...] = jnp.zeros_like(zeros_ref)

        def fill(act):
            def body(e, carry):
                _run_copies(fill_ref[1, e], zeros_ref, 0, xs_hbm, fill_ref[0, e], fill_sem, FILL_BITS, act,
                            advance_src=False)
                return carry
            lax.fori_loop(0, N_EXPERTS, body, 0)

        fill(_start)
        fill(_wait)

        first_tail = fill_ref[0, N_EXPERTS] // zrows
        n_chunks = xs_hbm.shape[0] // (zrows * ROW_TILES)

        def tail(act):
            def body(j, carry):
                act(pltpu.make_async_copy(zeros_ref, _rows(xs_hbm, j * zrows, zrows), fill_sem))
                return carry
            lax.fori_loop(first_tail, n_chunks, body, 0)

        tail(_start)
        tail(_wait)

    def drain(slot):
        pltpu.make_async_copy(sbuf.at[slot], _rows(xs_hbm, 0, SORT_ROWS), sems.at[slot]).wait()

    @pl.when(i >= 2)
    def _():
        drain(par)

    pos = pos_ref[0]
    pi = lax.broadcasted_iota(I32, (SORT_ROWS, tb), 0)
    onehot = jnp.where(pi == pos[0:1, :], 1.0, 0.0)
    for k in range(1, TOP_K):
        onehot = onehot + jnp.where(pi == pos[k:k + 1, :], 1.0, 0.0)
    xsorted = jnp.dot(onehot.astype(BF16), x_ref[...].astype(BF16), preferred_element_type=F32)
    buf = sbuf.at[par]
    _to_row_tiled(buf, xsorted)

    _segment_loop(tab_ref, i, lambda n, off, slot: _run_copies(n, buf, off, xs_hbm, slot, sems.at[par],
                                                               SEG_BITS, _start))

    @pl.when(i == nblk - 1)
    def _():
        drain(par)
        if nblk >= 2:
            drain(1 - par)


def _dispatch(tab, fill, pos, x, n_slots):
    nblk, _, tb = pos.shape
    d = x.shape[1]
    zrows = SLOT_BLOCK // 2
    vmem = _vmem_limit(2 * tb * d * 4, 2 * SORT_ROWS * d * 4, zrows * d * 4, 3 * SORT_ROWS * d * 4)
    grid_spec = pltpu.PrefetchScalarGridSpec(
        num_scalar_prefetch=2,
        grid=(nblk,),
        in_specs=[pl.BlockSpec((1, TOP_K, tb), lambda i, tab, fill: (i, 0, 0)),
                  pl.BlockSpec((tb, d), lambda i, tab, fill: (i, 0))],
        out_specs=pl.BlockSpec(memory_space=pl.ANY),
        scratch_shapes=[pltpu.VMEM((2, SORT_ROWS * ROW_TILES, LANES), F32),
                        pltpu.VMEM((zrows * ROW_TILES, LANES), F32),
                        pltpu.SemaphoreType.DMA((2,)),
                        pltpu.SemaphoreType.DMA(())],
    )
    return pl.pallas_call(
        functools.partial(_dispatch_kernel, nblk=nblk),
        grid_spec=grid_spec,
        out_shape=jax.ShapeDtypeStruct((n_slots * ROW_TILES, LANES), F32),
        compiler_params=pltpu.CompilerParams(dimension_semantics=("arbitrary",), has_side_effects=True,
                                             vmem_limit_bytes=vmem),
    )(tab, fill, pos, x)


def _expert_kernel(be_ref, nu_ref, xs_ref, w1_ref, b1_ref, w2_ref, b2_ref, y_ref, w1b_ref, w2b_ref):
    i = pl.program_id(0)
    used = i < nu_ref[0]

    @pl.when(used & ((i == 0) | (be_ref[i] != be_ref[jnp.maximum(i - 1, 0)])))
    def _():
        w1b_ref[...] = w1_ref[0].astype(BF16)
        w2b_ref[...] = w2_ref[0].astype(BF16)

    @pl.when(used)
    def _():
        xb = _from_row_tiled(xs_ref, SLOT_BLOCK).astype(BF16)
        h = jnp.dot(xb, w1b_ref[...], preferred_element_type=F32) + b1_ref[0]
        glu = jnp.minimum(h[:, :D_FF], SWIGLU_LIMIT)
        lin = jnp.clip(h[:, D_FF:], -SWIGLU_LIMIT, SWIGLU_LIMIT)
        act = glu / (1.0 + jnp.exp(-SWIGLU_ALPHA * glu)) * (lin + 1.0)
        _to_row_tiled(y_ref, jnp.dot(act.astype(BF16), w2b_ref[...], preferred_element_type=F32) + b2_ref[0])

    @pl.when(i >= nu_ref[0])
    def _():
        y_ref[...] = jnp.zeros_like(y_ref)


def _experts(layer, block_expert, n_used, xs, w1_all, b1_all, w2_all, b2_all):
    n_slots = xs.shape[0] // ROW_TILES
    bs = SLOT_BLOCK
    d = D_MODEL
    w_elems = d * 2 * D_FF + D_FF * d
    vmem = _vmem_limit(2 * bs * d * 4, 2 * w_elems * 4, w_elems * 2, 2 * bs * d * 4, 2 * bs * 2 * D_FF * 4)
    wmap = lambda i, be, nu: (layer, be[i], 0, 0)
    slot_blk = pl.BlockSpec((bs * ROW_TILES, LANES), lambda i, be, nu: (i, 0))
    grid_spec = pltpu.PrefetchScalarGridSpec(
        num_scalar_prefetch=2,
        grid=(n_slots // bs,),
        in_specs=[slot_blk,
                  pl.BlockSpec((None, 1, d, 2 * D_FF), wmap),
                  pl.BlockSpec((None, 1, 1, 2 * D_FF), wmap),
                  pl.BlockSpec((None, 1, D_FF, d), wmap),
                  pl.BlockSpec((None, 1, 1, d), wmap)],
        out_specs=slot_blk,
        scratch_shapes=[pltpu.VMEM((d, 2 * D_FF), BF16), pltpu.VMEM((D_FF, d), BF16)],
    )
    return pl.pallas_call(
        _expert_kernel,
        grid_spec=grid_spec,
        out_shape=jax.ShapeDtypeStruct((n_slots * ROW_TILES, LANES), F32),
        compiler_params=_params(vmem, 1),
    )(block_expert, n_used, xs, w1_all, b1_all, w2_all, b2_all)


def _combine_kernel(tab_ref, y_hbm, pos_ref, gates_ref, x_ref, g_ref, beta_ref, o_ref, ybuf, sems, *, nblk):
    i = pl.program_id(0)
    par = i % 2
    tb = x_ref.shape[0]

    def fetch(blk, slot):
        dst = ybuf.at[slot]
        _segment_loop(tab_ref, blk, lambda n, off, first: _run_copies(n, y_hbm, first, dst, off, sems.at[slot],
                                                                      SEG_BITS, _start))

    @pl.when(i == 0)
    def _():
        fetch(0, 0)

    @pl.when(i + 1 < nblk)
    def _():
        fetch(i + 1, 1 - par)

    pltpu.make_async_copy(_rows(y_hbm, 0, SORT_ROWS), ybuf.at[par], sems.at[par]).wait()
    ysorted = _from_row_tiled(ybuf.at[par], SORT_ROWS)

    pos = pos_ref[...]
    gates = gates_ref[...]
    pj = lax.broadcasted_iota(I32, (tb, SORT_ROWS), 1)
    wsel = jnp.where(pj == pos[:, 0:1], gates[:, 0:1], 0.0)
    for k in range(1, TOP_K):
        wsel = wsel + jnp.where(pj == pos[:, k:k + 1], gates[:, k:k + 1], 0.0)
    f = jnp.dot(wsel.astype(BF16), ysorted.astype(BF16), preferred_element_type=F32)
    o_ref[...] = _layer_norm_rows(DEEPNORM_ALPHA * x_ref[...] + f, g_ref[...], beta_ref[...])


def _combine_ln(tab, y, pos_tok, gates_tok, x, g, beta):
    t, d = x.shape
    tb = TOK_BLOCK
    nblk = t // tb
    vec = pl.BlockSpec((1, d), lambda i, tab: (0, 0))
    tok4 = pl.BlockSpec((tb, TOP_K), lambda i, tab: (i, 0))
    vmem = _vmem_limit(2 * SORT_ROWS * d * 4, 4 * tb * d * 4, 3 * SORT_ROWS * d * 4, 2 * tb * SORT_ROWS * 4)
    grid_spec = pltpu.PrefetchScalarGridSpec(
        num_scalar_prefetch=1,
        grid=(nblk,),
        in_specs=[pl.BlockSpec(memory_space=pl.ANY), tok4, tok4,
                  pl.BlockSpec((tb, d), lambda i, tab: (i, 0)),
                  vec, vec],
        out_specs=pl.BlockSpec((tb, d), lambda i, tab: (i, 0)),
        scratch_shapes=[pltpu.VMEM((2, SORT_ROWS * ROW_TILES, LANES), F32),
                        pltpu.SemaphoreType.DMA((2,))],
    )
    return pl.pallas_call(
        functools.partial(_combine_kernel, nblk=nblk),
        grid_spec=grid_spec,
        out_shape=jax.ShapeDtypeStruct((t, d), F32),
        compiler_params=_params(vmem, 1),
    )(tab, y, pos_tok, gates_tok, x, g, beta)


def _moe_ln(layer, x, w_router, b_router, w1_all, b1_all, w2_all, b2_all, g, beta):
    t, d = x.shape
    bs = SLOT_BLOCK
    n_blocks = (t * TOP_K) // bs + N_EXPERTS
    n_slots = n_blocks * bs

    pos, gates, nbe, base, cnt = _router(x, w_router.T, b_router.reshape(N_EXPERTS, 1))
    counts = cnt[:, 0].astype(I32)
    padded = (counts + bs - 1) // bs * bs
    padded_end = jnp.cumsum(padded)
    start_padded = (padded_end - padded).astype(I32)
    block_start = jnp.arange(n_blocks, dtype=I32) * bs
    block_expert = jnp.minimum(jnp.sum(padded_end[None, :] <= block_start[:, None], axis=1),
                               N_EXPERTS - 1).astype(I32)
    n_used = (padded_end[-1:] // bs).astype(I32)

    run_len = nbe[:, :, 0].astype(I32)
    run_off = jnp.cumsum(run_len, axis=1) - run_len
    run_slot = start_padded[None, :] + base[:, :, 0].astype(I32)
    tab = jnp.concatenate([run_len, run_off, run_slot], axis=1).reshape(-1)
    zero1 = jnp.zeros((1,), I32)
    fill = jnp.stack([jnp.concatenate([start_padded + counts, padded_end[-1:]]),
                      jnp.concatenate([padded - counts, zero1])]).astype(I32)

    xs = _dispatch(tab, fill, pos, x, n_slots)
    y = _experts(layer, block_expert, n_used, xs, w1_all, b1_all, w2_all, b2_all)
    pos_tok = pos.transpose(0, 2, 1).reshape(t, TOP_K)
    gates_tok = gates.transpose(0, 2, 1).reshape(t, TOP_K)
    return _combine_ln(tab, y, pos_tok, gates_tok, x, g, beta)


def kernel(x, w_ret_in, w_ret_out, w_swa_in, b_swa_in, w_swa_out, b_swa_out, swa_sinks, ln_mix_g, ln_mix_b,
           w_router, b_router, w_exp_in, b_exp_in, w_exp_out, b_exp_out, ln_ffn_g, ln_ffn_b):
    batch, seq, d = x.shape
    t = batch * seq
    xt = x.reshape(t, d)
    row = lambda v: v.reshape(1, -1)
    zeros_d = jnp.zeros((1, d), F32)
    w1_all, w2_all = w_exp_in, w_exp_out
    b1_all = b_exp_in.reshape(DEPTH, N_EXPERTS, 1, 2 * D_FF)
    b2_all = b_exp_out.reshape(DEPTH, N_EXPERTS, 1, d)
    for i in range(DEPTH):
        j = i // 2
        if i % 2 == 0:
            proj = _proj(xt, w_ret_in[j].astype(BF16), jnp.zeros((1, RET_IN), F32), PROJ_N_CHUNK)
            mixed = _retention_core(proj, batch, seq)
            w_out, b_out = w_ret_out[j].astype(BF16), zeros_d
        else:
            proj = _proj(xt, w_swa_in[j].astype(BF16), row(b_swa_in[j]), SWA_IN)
            mixed = _swa_core(proj, swa_sinks[j], batch, seq)
            w_out, b_out = w_swa_out[j].astype(BF16), row(b_swa_out[j])
        xt = _outproj_ln(mixed, w_out, b_out, xt, row(ln_mix_g[i]), row(ln_mix_b[i]))
        xt = _moe_ln(i, xt, w_router[i], b_router[i], w1_all, b1_all, w2_all, b2_all,
                     row(ln_ffn_g[i]), row(ln_ffn_b[i]))
    return xt.reshape(batch, seq, d)
```

```python
import functools
import math

import jax
import jax.numpy as jnp
from jax import lax
from jax.experimental import pallas as pl
from jax.experimental.pallas import tpu as pltpu

F32 = jnp.float32
BF16 = jnp.bfloat16
I32 = jnp.int32
U32 = jnp.uint32

D_MODEL = 1024
DEPTH = 2
RET_HEADS = 4
RET_QK_DIM = 256
RET_V_DIM = 512
RET_CHUNK = 128
RET_QK_ALL = RET_HEADS * RET_QK_DIM
RET_V_ALL = RET_HEADS * RET_V_DIM
RET_IN = 2 * RET_QK_ALL + 2 * RET_V_ALL
SWA_Q_HEADS = 16
SWA_KV_HEADS = 2
SWA_GROUP = 8
SWA_HEAD_DIM = 64
SWA_BLOCK = 128
WINDOW = 128
SWA_Q_ALL = SWA_Q_HEADS * SWA_HEAD_DIM
SWA_KV_ALL = 2 * SWA_KV_HEADS * SWA_HEAD_DIM
SWA_IN = SWA_Q_ALL + SWA_KV_ALL
N_EXPERTS = 32
TOP_K = 4
D_FF = 1024
SWIGLU_ALPHA = 1.702
SWIGLU_LIMIT = 7.0
LN_EPS = 1e-5
GN_EPS = 1e-6
DEEPNORM_ALPHA = (2 * DEPTH) ** 0.25

V7X_VMEM_BYTES = 64 * 1024 * 1024
VMEM_CAP = V7X_VMEM_BYTES - 8 * 1024 * 1024
ROW_TILE = 512
RET_KERNEL_CHUNK = 256
PROJ_N_CHUNK = 1536
TOK_BLOCK = 256
ROUTER_BLOCKS = 4
SLOT_BLOCK = 512
SUBLANES = 8
LANES = 128
ROW_TILES = D_MODEL // LANES
assert ROW_TILES == SUBLANES
SORT_ROWS = TOP_K * TOK_BLOCK
SEG_BITS = tuple(1 << b for b in range(TOK_BLOCK.bit_length() - 1, -1, -1))
FILL_BITS = tuple(1 << b for b in range(SLOT_BLOCK.bit_length() - 2, -1, -1))
SEG_TAB = 3 * N_EXPERTS

NT_DIMS = (((1,), (1,)), ((), ()))
TN_DIMS = (((0,), (0,)), ((), ()))


def _vmem_limit(*nbytes):
    est = int(sum(nbytes) * 1.2) + (4 << 20)
    return min(max(est, 16 << 20), VMEM_CAP)


def _params(vmem, n_grid):
    return pltpu.CompilerParams(dimension_semantics=("arbitrary",) * n_grid,
                                vmem_limit_bytes=vmem)


def _layer_norm_rows(y, g, b):
    mu = jnp.mean(y, axis=-1, keepdims=True)
    yc = y - mu
    var = jnp.mean(yc * yc, axis=-1, keepdims=True)
    return yc * lax.rsqrt(var + LN_EPS) * g + b


def _proj_kernel(x_ref, w_ref, b_ref, o_ref, *, n_chunk):
    xb = x_ref[...].astype(BF16)
    n_out = o_ref.shape[1]
    for c in range(0, n_out, n_chunk):
        acc = jnp.dot(xb, w_ref[:, c:c + n_chunk], preferred_element_type=F32)
        o_ref[:, c:c + n_chunk] = (acc + b_ref[:, c:c + n_chunk]).astype(o_ref.dtype)


def _proj(x, w_bf16, bias, n_chunk):
    t, d = x.shape
    n = w_bf16.shape[1]
    tm = ROW_TILE
    vmem = _vmem_limit(2 * tm * d * 4, 2 * d * n * 2, 2 * tm * n * 2, tm * n_chunk * 4 * 2, tm * d * 2)
    return pl.pallas_call(
        functools.partial(_proj_kernel, n_chunk=n_chunk),
        grid=(t // tm,),
        in_specs=[pl.BlockSpec((tm, d), lambda i: (i, 0)),
                  pl.BlockSpec((d, n), lambda i: (0, 0)),
                  pl.BlockSpec((1, n), lambda i: (0, 0))],
        out_specs=pl.BlockSpec((tm, n), lambda i: (i, 0)),
        out_shape=jax.ShapeDtypeStruct((t, n), BF16),
        compiler_params=_params(vmem, 1),
    )(x, w_bf16, bias)


def _ret_gammas():
    return [1.0 - 2.0 ** (-5.0 - h) for h in range(RET_HEADS)]


def _ret_kernel(q_ref, k_ref, v_ref, g_ref, dm_ref, qd_ref, kd_ref, o_ref, state_ref):
    @pl.when(pl.program_id(1) == 0)
    def _():
        state_ref[...] = jnp.zeros_like(state_ref)

    gammas = _ret_gammas()
    heads = range(RET_HEADS)
    qs = [q_ref[:, h * RET_QK_DIM:(h + 1) * RET_QK_DIM] for h in heads]
    ks = [k_ref[:, h * RET_QK_DIM:(h + 1) * RET_QK_DIM] for h in heads]
    vs = [v_ref[:, h * RET_V_DIM:(h + 1) * RET_V_DIM] for h in heads]
    scores = [(lax.dot_general(qs[h], ks[h], NT_DIMS, preferred_element_type=F32) * dm_ref[h]).astype(BF16)
              for h in heads]
    inter = [jnp.dot(qs[h], state_ref[h].astype(BF16), preferred_element_type=F32) * qd_ref[h] for h in heads]
    outs = [jnp.dot(scores[h], vs[h], preferred_element_type=F32) + inter[h] for h in heads]
    for h in heads:
        kdec = (ks[h].astype(F32) * kd_ref[h]).astype(BF16)
        upd = lax.dot_general(kdec, vs[h], TN_DIMS, preferred_element_type=F32)
        state_ref[h] = state_ref[h] * (gammas[h] ** q_ref.shape[0]) + upd
    for h in heads:
        o = outs[h]
        mu = jnp.mean(o, axis=-1, keepdims=True)
        oc = o - mu
        var = jnp.mean(oc * oc, axis=-1, keepdims=True)
        on = oc * lax.rsqrt(var + GN_EPS)
        gh = g_ref[:, h * RET_V_DIM:(h + 1) * RET_V_DIM].astype(F32)
        gate = gh / (1.0 + jnp.exp(-gh))
        o_ref[:, h * RET_V_DIM:(h + 1) * RET_V_DIM] = (gate * on).astype(o_ref.dtype)


def _retention_core(proj, batch, seq):
    c = RET_KERNEL_CHUNK
    nc = seq // c
    log_g = jnp.log(jnp.asarray(_ret_gammas(), F32))
    pos = jnp.arange(c, dtype=F32)
    rel = pos[:, None] - pos[None, :]
    scale = RET_QK_DIM ** -0.5
    dm = jnp.where(rel >= 0, jnp.exp(log_g[:, None, None] * jnp.maximum(rel, 0.0)), 0.0) * scale
    qd = jnp.exp(log_g[:, None] * (pos + 1.0))[..., None]
    kd = jnp.exp(log_g[:, None] * (c - 1.0 - pos))[..., None] * scale
    t = batch * seq
    row = lambda b, i: b * nc + i
    const3 = lambda b, i: (0, 0, 0)
    vmem = _vmem_limit(2 * c * RET_IN * 2, 2 * c * RET_V_ALL * 2, RET_HEADS * RET_QK_DIM * RET_V_DIM * 4 * 2,
                       8 << 20)
    return pl.pallas_call(
        _ret_kernel,
        grid=(batch, nc),
        in_specs=[pl.BlockSpec((c, RET_QK_ALL), lambda b, i: (row(b, i), 0)),
                  pl.BlockSpec((c, RET_QK_ALL), lambda b, i: (row(b, i), 1)),
                  pl.BlockSpec((c, RET_V_ALL), lambda b, i: (row(b, i), 1)),
                  pl.BlockSpec((c, RET_V_ALL), lambda b, i: (row(b, i), 2)),
                  pl.BlockSpec((RET_HEADS, c, c), const3),
                  pl.BlockSpec((RET_HEADS, c, 1), const3),
                  pl.BlockSpec((RET_HEADS, c, 1), const3)],
        out_specs=pl.BlockSpec((c, RET_V_ALL), lambda b, i: (row(b, i), 0)),
        out_shape=jax.ShapeDtypeStruct((t, RET_V_ALL), BF16),
        scratch_shapes=[pltpu.VMEM((RET_HEADS, RET_QK_DIM, RET_V_DIM), F32)],
        compiler_params=_params(vmem, 2),
    )(proj, proj, proj, proj, dm, qd, kd)


def _swa_kernel(sink_ref, bias_ref, q_ref, kvp_ref, kvc_ref, o_ref):
    c = SWA_BLOCK
    hd = SWA_HEAD_DIM
    qi = lax.broadcasted_iota(I32, (c, c), 0)
    kj = lax.broadcasted_iota(I32, (c, c), 1)
    cur = kj <= qi
    first_head = lax.broadcasted_iota(I32, (c, 2 * hd), 1) < hd
    ones = jnp.ones((2 * c, hd), BF16)
    zeros = jnp.zeros((2 * c, hd), BF16)
    scores, v_pairs = [], []
    for j in range(SWA_KV_HEADS):
        kcat = jnp.concatenate([kvp_ref[:, j * hd:(j + 1) * hd], kvc_ref[:, j * hd:(j + 1) * hd]], axis=0)
        k_t = kcat.astype(F32).T.astype(BF16)
        v0 = SWA_KV_HEADS * hd + j * hd
        vcat = jnp.concatenate([kvp_ref[:, v0:v0 + hd], kvc_ref[:, v0:v0 + hd]], axis=0)
        v_pairs.append(jnp.concatenate([jnp.concatenate([vcat, zeros, ones, zeros], axis=1),
                                        jnp.concatenate([zeros, vcat, zeros, ones], axis=1)], axis=0))
        for g in range(SWA_GROUP):
            h = j * SWA_GROUP + g
            qh = q_ref[:, h * hd:(h + 1) * hd] * (hd ** -0.5)
            s2 = jnp.dot(qh, k_t, preferred_element_type=F32)
            scores.append(jnp.where(cur, s2[:, c:], s2[:, :c]) + bias_ref[h])

    p2s, tails = [], []
    for h in range(SWA_Q_HEADS):
        sink = sink_ref[h]
        m = jnp.maximum(jnp.max(scores[h], axis=-1, keepdims=True), sink)
        p = jnp.exp(scores[h] - m)
        tails.append(jnp.exp(sink - m))
        p2s.append(jnp.concatenate([jnp.where(cur, 0.0, p), jnp.where(cur, p, 0.0)], axis=1).astype(BF16))

    outs = []
    for h in range(0, SWA_Q_HEADS, 2):
        ov = jnp.dot(jnp.concatenate(p2s[h:h + 2], axis=1), v_pairs[h // SWA_GROUP],
                     preferred_element_type=F32)
        den = ov[:, 2 * hd:] + jnp.where(first_head, tails[h], tails[h + 1])
        outs.append(ov[:, :2 * hd] / den)
    o_ref[...] = jnp.concatenate(outs, axis=-1).astype(o_ref.dtype)


def _swa_core(proj, sinks, batch, seq):
    c = SWA_BLOCK
    nb = seq // c
    t = batch * seq
    kv_col = SWA_Q_ALL // SWA_KV_ALL
    assert WINDOW == c
    qi = jnp.arange(c)[:, None]
    kj = jnp.arange(c)[None, :]
    cur = kj <= qi
    dist = jnp.where(cur, qi - kj, qi - kj + c).astype(F32)
    visible = jnp.stack([cur, jnp.ones_like(cur)])
    slopes = 2.0 ** (-8.0 * jnp.arange(1, SWA_Q_HEADS + 1, dtype=F32) / SWA_Q_HEADS)
    bias = jnp.where(visible[:, None], -slopes[None, :, None, None] * dist[None, None], -jnp.inf)
    vmem = _vmem_limit(2 * c * SWA_IN * 2 * 2, 2 * SWA_Q_HEADS * c * c * 4, 16 << 20)
    return pl.pallas_call(
        _swa_kernel,
        grid=(batch, nb),
        in_specs=[pl.BlockSpec(memory_space=pltpu.SMEM),
                  pl.BlockSpec((None, SWA_Q_HEADS, c, c), lambda b, n: (jnp.minimum(n, 1), 0, 0, 0)),
                  pl.BlockSpec((c, SWA_Q_ALL), lambda b, n: (b * nb + n, 0)),
                  pl.BlockSpec((c, SWA_KV_ALL), lambda b, n: (b * nb + jnp.maximum(n - 1, 0), kv_col)),
                  pl.BlockSpec((c, SWA_KV_ALL), lambda b, n: (b * nb + n, kv_col))],
        out_specs=pl.BlockSpec((c, SWA_Q_ALL), lambda b, n: (b * nb + n, 0)),
        out_shape=jax.ShapeDtypeStruct((t, SWA_Q_ALL), BF16),
        compiler_params=_params(vmem, 2),
    )(sinks, bias.astype(F32), proj, proj, proj)


def _outproj_ln_kernel(a_ref, w_ref, b_ref, x_ref, g_ref, beta_ref, o_ref):
    half = a_ref.shape[0] // 2
    for r in (0, half):
        m = jnp.dot(a_ref[r:r + half, :], w_ref[...], preferred_element_type=F32) + b_ref[...]
        o_ref[r:r + half, :] = _layer_norm_rows(DEEPNORM_ALPHA * x_ref[r:r + half, :] + m, g_ref[...],
                                                beta_ref[...])


def _outproj_ln(a, w_bf16, bias, x, g, beta):
    t, kin = a.shape
    d = D_MODEL
    tm = ROW_TILE
    vec = pl.BlockSpec((1, d), lambda i: (0, 0))
    vmem = _vmem_limit(2 * tm * kin * 2, 2 * kin * d * 2, 4 * tm * d * 4, 4 * tm * d * 4)
    return pl.pallas_call(
        _outproj_ln_kernel,
        grid=(t // tm,),
        in_specs=[pl.BlockSpec((tm, kin), lambda i: (i, 0)),
                  pl.BlockSpec((kin, d), lambda i: (0, 0)),
                  vec,
                  pl.BlockSpec((tm, d), lambda i: (i, 0)),
                  vec, vec],
        out_specs=pl.BlockSpec((tm, d), lambda i: (i, 0)),
        out_shape=jax.ShapeDtypeStruct((t, d), F32),
        compiler_params=_params(vmem, 1),
    )(a, w_bf16, bias, x, g, beta)


def _router_kernel(x_ref, wt_ref, b_ref, pos_ref, gate_ref, nbe_ref, base_ref, cnt_ref, carry_ref):
    @pl.when(pl.program_id(0) == 0)
    def _():
        carry_ref[...] = jnp.zeros_like(carry_ref)

    tb = TOK_BLOCK
    wt = wt_ref[...]
    wh = wt.astype(BF16)
    wl = (wt - wh.astype(F32)).astype(BF16)
    nt = x_ref.shape[0]
    x = x_ref[...]
    xh = x.astype(BF16)
    xl = (x - xh.astype(F32)).astype(BF16)
    logits = (lax.dot_general(wh, xh, NT_DIMS, preferred_element_type=F32)
              + lax.dot_general(wh, xl, NT_DIMS, preferred_element_type=F32)
              + lax.dot_general(wl, xh, NT_DIMS, preferred_element_type=F32)
              + b_ref[...])
    eidx = lax.broadcasted_iota(I32, (N_EXPERTS, nt), 0).astype(F32)
    work = logits
    sels, vals = [], []
    for _ in range(TOP_K):
        m = jnp.max(work, axis=0, keepdims=True)
        idx = jnp.min(jnp.where(work == m, eidx, float(N_EXPERTS)), axis=0, keepdims=True)
        sel = eidx == idx
        sels.append(sel)
        vals.append(m)
        work = jnp.where(sel, -jnp.inf, work)

    exps = [jnp.exp(v - vals[0]) for v in vals]
    den = exps[0] + exps[1] + exps[2] + exps[3]
    gates = jnp.concatenate([e / den for e in exps], axis=0)

    mask = jnp.zeros((N_EXPERTS, nt), F32)
    for sel in sels:
        mask = mask + jnp.where(sel, 1.0, 0.0)

    ti = lax.broadcasted_iota(I32, (tb, tb), 0)
    tj = lax.broadcasted_iota(I32, (tb, tb), 1)
    upper = jnp.where(ti < tj, 1.0, 0.0).astype(BF16)
    ei = lax.broadcasted_iota(I32, (N_EXPERTS, N_EXPERTS), 0)
    ej = lax.broadcasted_iota(I32, (N_EXPERTS, N_EXPERTS), 1)
    lower = jnp.where(ej < ei, 1.0, 0.0).astype(BF16)
    carry = carry_ref[...]

    for u in range(ROUTER_BLOCKS):
        blk = slice(u * tb, (u + 1) * tb)
        mask_u = mask[:, blk]
        rank_loc = jnp.dot(mask_u.astype(BF16), upper, preferred_element_type=F32)
        n_be = jnp.broadcast_to(jnp.sum(mask_u, axis=1, keepdims=True), (N_EXPERTS, LANES))
        off_be = jnp.dot(lower, n_be.astype(BF16), preferred_element_type=F32)
        pos_all = rank_loc + off_be[:, :1]
        poss = [jnp.sum(jnp.where(sel[:, blk], pos_all, 0.0), axis=0, keepdims=True) for sel in sels]

        pos_ref[u] = jnp.concatenate(poss, axis=0).astype(I32)
        gate_ref[u] = gates[:, blk]
        nbe_ref[u] = n_be
        base_ref[u] = carry
        carry = carry + n_be

    carry_ref[...] = carry
    cnt_ref[...] = carry


def _router(x, w_router_t, b_router_col):
    t, d = x.shape
    tb = TOK_BLOCK
    rb = ROUTER_BLOCKS
    assert tb <= 256, "per-block expert counts must stay exact in bf16"
    nblk = t // tb
    blk3 = pl.BlockSpec((rb, TOP_K, tb), lambda i: (i, 0, 0))
    seg3 = pl.BlockSpec((rb, N_EXPERTS, LANES), lambda i: (i, 0, 0))
    vmem = _vmem_limit(2 * rb * tb * d * 4, 2 * N_EXPERTS * d * 4, 8 << 20)
    return pl.pallas_call(
        _router_kernel,
        grid=(nblk // rb,),
        in_specs=[pl.BlockSpec((rb * tb, d), lambda i: (i, 0)),
                  pl.BlockSpec((N_EXPERTS, d), lambda i: (0, 0)),
                  pl.BlockSpec((N_EXPERTS, 1), lambda i: (0, 0))],
        out_specs=[blk3, blk3, seg3, seg3, pl.BlockSpec((N_EXPERTS, LANES), lambda i: (0, 0))],
        out_shape=[jax.ShapeDtypeStruct((nblk, TOP_K, tb), I32),
                   jax.ShapeDtypeStruct((nblk, TOP_K, tb), F32),
                   jax.ShapeDtypeStruct((nblk, N_EXPERTS, LANES), F32),
                   jax.ShapeDtypeStruct((nblk, N_EXPERTS, LANES), F32),
                   jax.ShapeDtypeStruct((N_EXPERTS, LANES), F32)],
        scratch_shapes=[pltpu.VMEM((N_EXPERTS, LANES), F32)],
        compiler_params=_params(vmem, 1),
    )(x, w_router_t, b_router_col)


def _rows(ref, row, n):
    return ref.at[pl.ds(pl.multiple_of(row * ROW_TILES, ROW_TILES), n * ROW_TILES)]


def _to_row_tiled(ref, value):
    n = value.shape[0]
    for c in range(ROW_TILES):
        ref[pl.ds(c, n, stride=ROW_TILES), :] = value[:, c * LANES:(c + 1) * LANES]


def _from_row_tiled(ref, n):
    return jnp.concatenate([ref[pl.ds(c, n, stride=ROW_TILES), :] for c in range(ROW_TILES)], axis=1)


def _run_copies(n, src_ref, src_row, dst_ref, dst_row, sem, bits, act, advance_src=True):
    for bit in bits:
        @pl.when((n & bit) != 0)
        def _():
            act(pltpu.make_async_copy(_rows(src_ref, src_row, bit), _rows(dst_ref, dst_row, bit), sem))
        if advance_src:
            src_row = src_row + (n & bit)
        dst_row = dst_row + (n & bit)


def _start(cp):
    cp.start()


def _wait(cp):
    cp.wait()


def _segment_loop(tab_ref, blk, act_on_run):
    base = blk * SEG_TAB

    def body(e, carry):
        act_on_run(tab_ref[base + e], tab_ref[base + N_EXPERTS + e], tab_ref[base + 2 * N_EXPERTS + e])
        return carry

    lax.fori_loop(0, N_EXPERTS, body, 0)


def _dispatch_kernel(tab_ref, fill_ref, pos_ref, x_ref, xs_hbm, sbuf, zeros_ref, sems, fill_sem, *, nblk):
    i = pl.program_id(0)
    par = i % 2
    tb = x_ref.shape[0]
    zrows = zeros_ref.shape[0] // ROW_TILES

    @pl.when(i == 0)
    def _():
        zeros_ref[...] = jnp.zeros_like(zeros_ref)

        def fill(act):
            def body(e, carry):
                _run_copies(fill_ref[1, e], zeros_ref, 0, xs_hbm, fill_ref[0, e], fill_sem, FILL_BITS, act,
                            advance_src=False)
                return carry
            lax.fori_loop(0, N_EXPERTS, body, 0)

        fill(_start)
        fill(_wait)

        first_tail = fill_ref[0, N_EXPERTS] // zrows
        n_chunks = xs_hbm.shape[0] // (zrows * ROW_TILES)

        def tail(act):
            def body(j, carry):
                act(pltpu.make_async_copy(zeros_ref, _rows(xs_hbm, j * zrows, zrows), fill_sem))
                return carry
            lax.fori_loop(first_tail, n_chunks, body, 0)

        tail(_start)
        tail(_wait)

    def drain(slot):
        pltpu.make_async_copy(sbuf.at[slot], _rows(xs_hbm, 0, SORT_ROWS), sems.at[slot]).wait()

    @pl.when(i >= 2)
    def _():
        drain(par)

    pos = pos_ref[0]
    pi = lax.broadcasted_iota(I32, (SORT_ROWS, tb), 0)
    onehot = jnp.where(pi == pos[0:1, :], 1.0, 0.0)
    for k in range(1, TOP_K):
        onehot = onehot + jnp.where(pi == pos[k:k + 1, :], 1.0, 0.0)
    xsorted = jnp.dot(onehot.astype(BF16), x_ref[...].astype(BF16), preferred_element_type=F32)
    buf = sbuf.at[par]
    _to_row_tiled(buf, xsorted)

    _segment_loop(tab_ref, i, lambda n, off, slot: _run_copies(n, buf, off, xs_hbm, slot, sems.at[par],
                                                               SEG_BITS, _start))

    @pl.when(i == nblk - 1)
    def _():
        drain(par)
        if nblk >= 2:
            drain(1 - par)


def _dispatch(tab, fill, pos, x, n_slots):
    nblk, _, tb = pos.shape
    d = x.shape[1]
    zrows = SLOT_BLOCK // 2
    vmem = _vmem_limit(2 * tb * d * 4, 2 * SORT_ROWS * d * 4, zrows * d * 4, 3 * SORT_ROWS * d * 4)
    grid_spec = pltpu.PrefetchScalarGridSpec(
        num_scalar_prefetch=2,
        grid=(nblk,),
        in_specs=[pl.BlockSpec((1, TOP_K, tb), lambda i, tab, fill: (i, 0, 0)),
                  pl.BlockSpec((tb, d), lambda i, tab, fill: (i, 0))],
        out_specs=pl.BlockSpec(memory_space=pl.ANY),
        scratch_shapes=[pltpu.VMEM((2, SORT_ROWS * ROW_TILES, LANES), F32),
                        pltpu.VMEM((zrows * ROW_TILES, LANES), F32),
                        pltpu.SemaphoreType.DMA((2,)),
                        pltpu.SemaphoreType.DMA(())],
    )
    return pl.pallas_call(
        functools.partial(_dispatch_kernel, nblk=nblk),
        grid_spec=grid_spec,
        out_shape=jax.ShapeDtypeStruct((n_slots * ROW_TILES, LANES), F32),
        compiler_params=pltpu.CompilerParams(dimension_semantics=("arbitrary",), has_side_effects=True,
                                             vmem_limit_bytes=vmem),
    )(tab, fill, pos, x)


def _expert_kernel(be_ref, nu_ref, xs_ref, w1_ref, b1_ref, w2_ref, b2_ref, y_ref, w1b_ref, w2b_ref):
    i = pl.program_id(0)
    used = i < nu_ref[0]

    @pl.when(used & ((i == 0) | (be_ref[i] != be_ref[jnp.maximum(i - 1, 0)])))
    def _():
        w1b_ref[...] = w1_ref[0].astype(BF16)
        w2b_ref[...] = w2_ref[0].astype(BF16)

    @pl.when(used)
    def _():
        xb = _from_row_tiled(xs_ref, SLOT_BLOCK).astype(BF16)
        h = jnp.dot(xb, w1b_ref[...], preferred_element_type=F32) + b1_ref[0]
        glu = jnp.minimum(h[:, :D_FF], SWIGLU_LIMIT)
        lin = jnp.clip(h[:, D_FF:], -SWIGLU_LIMIT, SWIGLU_LIMIT)
        act = glu / (1.0 + jnp.exp(-SWIGLU_ALPHA * glu)) * (lin + 1.0)
        _to_row_tiled(y_ref, jnp.dot(act.astype(BF16), w2b_ref[...], preferred_element_type=F32) + b2_ref[0])

    @pl.when(i >= nu_ref[0])
    def _():
        y_ref[...] = jnp.zeros_like(y_ref)


def _experts(layer, block_expert, n_used, xs, w1_all, b1_all, w2_all, b2_all):
    n_slots = xs.shape[0] // ROW_TILES
    bs = SLOT_BLOCK
    d = D_MODEL
    w_elems = d * 2 * D_FF + D_FF * d
    vmem = _vmem_limit(2 * bs * d * 4, 2 * w_elems * 4, w_elems * 2, 2 * bs * d * 4, 2 * bs * 2 * D_FF * 4)
    wmap = lambda i, be, nu: (layer, be[i], 0, 0)
    slot_blk = pl.BlockSpec((bs * ROW_TILES, LANES), lambda i, be, nu: (i, 0))
    grid_spec = pltpu.PrefetchScalarGridSpec(
        num_scalar_prefetch=2,
        grid=(n_slots // bs,),
        in_specs=[slot_blk,
                  pl.BlockSpec((None, 1, d, 2 * D_FF), wmap),
                  pl.BlockSpec((None, 1, 1, 2 * D_FF), wmap),
                  pl.BlockSpec((None, 1, D_FF, d), wmap),
                  pl.BlockSpec((None, 1, 1, d), wmap)],
        out_specs=slot_blk,
        scratch_shapes=[pltpu.VMEM((d, 2 * D_FF), BF16), pltpu.VMEM((D_FF, d), BF16)],
    )
    return pl.pallas_call(
        _expert_kernel,
        grid_spec=grid_spec,
        out_shape=jax.ShapeDtypeStruct((n_slots * ROW_TILES, LANES), F32),
        compiler_params=_params(vmem, 1),
    )(block_expert, n_used, xs, w1_all, b1_all, w2_all, b2_all)


def _combine_kernel(tab_ref, y_hbm, pos_ref, gates_ref, x_ref, g_ref, beta_ref, o_ref, ybuf, sems, *, nblk):
    i = pl.program_id(0)
    par = i % 2
    tb = x_ref.shape[0]

    def fetch(blk, slot):
        dst = ybuf.at[slot]
        _segment_loop(tab_ref, blk, lambda n, off, first: _run_copies(n, y_hbm, first, dst, off, sems.at[slot],
                                                                      SEG_BITS, _start))

    @pl.when(i == 0)
    def _():
        fetch(0, 0)

    @pl.when(i + 1 < nblk)
    def _():
        fetch(i + 1, 1 - par)

    pltpu.make_async_copy(_rows(y_hbm, 0, SORT_ROWS), ybuf.at[par], sems.at[par]).wait()
    ysorted = _from_row_tiled(ybuf.at[par], SORT_ROWS)

    pos = pos_ref[...]
    gates = gates_ref[...]
    pj = lax.broadcasted_iota(I32, (tb, SORT_ROWS), 1)
    wsel = jnp.where(pj == pos[:, 0:1], gates[:, 0:1], 0.0)
    for k in range(1, TOP_K):
        wsel = wsel + jnp.where(pj == pos[:, k:k + 1], gates[:, k:k + 1], 0.0)
    f = jnp.dot(wsel.astype(BF16), ysorted.astype(BF16), preferred_element_type=F32)
    o_ref[...] = _layer_norm_rows(DEEPNORM_ALPHA * x_ref[...] + f, g_ref[...], beta_ref[...])


def _combine_ln(tab, y, pos_tok, gates_tok, x, g, beta):
    t, d = x.shape
    tb = TOK_BLOCK
    nblk = t // tb
    vec = pl.BlockSpec((1, d), lambda i, tab: (0, 0))
    tok4 = pl.BlockSpec((tb, TOP_K), lambda i, tab: (i, 0))
    vmem = _vmem_limit(2 * SORT_ROWS * d * 4, 4 * tb * d * 4, 3 * SORT_ROWS * d * 4, 2 * tb * SORT_ROWS * 4)
    grid_spec = pltpu.PrefetchScalarGridSpec(
        num_scalar_prefetch=1,
        grid=(nblk,),
        in_specs=[pl.BlockSpec(memory_space=pl.ANY), tok4, tok4,
                  pl.BlockSpec((tb, d), lambda i, tab: (i, 0)),
                  vec, vec],
        out_specs=pl.BlockSpec((tb, d), lambda i, tab: (i, 0)),
        scratch_shapes=[pltpu.VMEM((2, SORT_ROWS * ROW_TILES, LANES), F32),
                        pltpu.SemaphoreType.DMA((2,))],
    )
    return pl.pallas_call(
        functools.partial(_combine_kernel, nblk=nblk),
        grid_spec=grid_spec,
        out_shape=jax.ShapeDtypeStruct((t, d), F32),
        compiler_params=_params(vmem, 1),
    )(tab, y, pos_tok, gates_tok, x, g, beta)


def _moe_ln(layer, x, w_router, b_router, w1_all, b1_all, w2_all, b2_all, g, beta):
    t, d = x.shape
    bs = SLOT_BLOCK
    n_blocks = (t * TOP_K) // bs + N_EXPERTS
    n_slots = n_blocks * bs

    pos, gates, nbe, base, cnt = _router(x, w_router.T, b_router.reshape(N_EXPERTS, 1))
    counts = cnt[:, 0].astype(I32)
    padded = (counts + bs - 1) // bs * bs
    padded_end = jnp.cumsum(padded)
    start_padded = (padded_end - padded).astype(I32)
    block_start = jnp.arange(n_blocks, dtype=I32) * bs
    block_expert = jnp.minimum(jnp.sum(padded_end[None, :] <= block_start[:, None], axis=1),
                               N_EXPERTS - 1).astype(I32)
    n_used = (padded_end[-1:] // bs).astype(I32)

    run_len = nbe[:, :, 0].astype(I32)
    run_off = jnp.cumsum(run_len, axis=1) - run_len
    run_slot = start_padded[None, :] + base[:, :, 0].astype(I32)
    tab = jnp.concatenate([run_len, run_off, run_slot], axis=1).reshape(-1)
    zero1 = jnp.zeros((1,), I32)
    fill = jnp.stack([jnp.concatenate([start_padded + counts, padded_end[-1:]]),
                      jnp.concatenate([padded - counts, zero1])]).astype(I32)

    xs = _dispatch(tab, fill, pos, x, n_slots)
    y = _experts(layer, block_expert, n_used, xs, w1_all, b1_all, w2_all, b2_all)
    pos_tok = pos.transpose(0, 2, 1).reshape(t, TOP_K)
    gates_tok = gates.transpose(0, 2, 1).reshape(t, TOP_K)
    return _combine_ln(tab, y, pos_tok, gates_tok, x, g, beta)


def kernel(x, w_ret_in, w_ret_out, w_swa_in, b_swa_in, w_swa_out, b_swa_out, swa_sinks, ln_mix_g, ln_mix_b,
           w_router, b_router, w_exp_in, b_exp_in, w_exp_out, b_exp_out, ln_ffn_g, ln_ffn_b):
    batch, seq, d = x.shape
    t = batch * seq
    xt = x.reshape(t, d)
    row = lambda v: v.reshape(1, -1)
    zeros_d = jnp.zeros((1, d), F32)
    w1_all, w2_all = w_exp_in, w_exp_out
    b1_all = b_exp_in.reshape(DEPTH, N_EXPERTS, 1, 2 * D_FF)
    b2_all = b_exp_out.reshape(DEPTH, N_EXPERTS, 1, d)
    for i in range(DEPTH):
        j = i // 2
        if i % 2 == 0:
            proj = _proj(xt, w_ret_in[j].astype(BF16), jnp.zeros((1, RET_IN), F32), PROJ_N_CHUNK)
            mixed = _retention_core(proj, batch, seq)
            w_out, b_out = w_ret_out[j].astype(BF16), zeros_d
        else:
            proj = _proj(xt, w_swa_in[j].astype(BF16), row(b_swa_in[j]), SWA_IN)
            mixed = _swa_core(proj, swa_sinks[j], batch, seq)
            w_out, b_out = w_swa_out[j].astype(BF16), row(b_swa_out[j])
        xt = _outproj_ln(mixed, w_out, b_out, xt, row(ln_mix_g[i]), row(ln_mix_b[i]))
        xt = _moe_ln(i, xt, w_router[i], b_router[i], w1_all, b1_all, w2_all, b2_all,
                     row(ln_ffn_g[i]), row(ln_ffn_b[i]))
    return xt.reshape(batch, seq, d)
```

```python
import functools
import math

import jax
import jax.numpy as jnp
from jax import lax
from jax.experimental import pallas as pl
from jax.experimental.pallas import tpu as pltpu

F32 = jnp.float32
BF16 = jnp.bfloat16
I32 = jnp.int32
U32 = jnp.uint32

D_MODEL = 1024
DEPTH = 2
RET_HEADS = 4
RET_QK_DIM = 256
RET_V_DIM = 512
RET_CHUNK = 128
RET_QK_ALL = RET_HEADS * RET_QK_DIM
RET_V_ALL = RET_HEADS * RET_V_DIM
RET_IN = 2 * RET_QK_ALL + 2 * RET_V_ALL
SWA_Q_HEADS = 16
SWA_KV_HEADS = 2
SWA_GROUP = 8
SWA_HEAD_DIM = 64
SWA_BLOCK = 128
WINDOW = 128
SWA_Q_ALL = SWA_Q_HEADS * SWA_HEAD_DIM
SWA_KV_ALL = 2 * SWA_KV_HEADS * SWA_HEAD_DIM
SWA_IN = SWA_Q_ALL + SWA_KV_ALL
N_EXPERTS = 32
TOP_K = 4
D_FF = 1024
SWIGLU_ALPHA = 1.702
SWIGLU_LIMIT = 7.0
LN_EPS = 1e-5
GN_EPS = 1e-6
DEEPNORM_ALPHA = (2 * DEPTH) ** 0.25

V7X_VMEM_BYTES = 64 * 1024 * 1024
VMEM_CAP = V7X_VMEM_BYTES - 8 * 1024 * 1024
ROW_TILE = 512
RET_KERNEL_CHUNK = 256
PROJ_N_CHUNK = 1536
TOK_BLOCK = 256
ROUTER_BLOCKS = 4
SLOT_BLOCK = 512
SUBLANES = 8
LANES = 128
ROW_TILES = D_MODEL // LANES
assert ROW_TILES == SUBLANES
SORT_ROWS = TOP_K * TOK_BLOCK
SEG_BITS = tuple(1 << b for b in range(TOK_BLOCK.bit_length() - 1, -1, -1))
FILL_BITS = tuple(1 << b for b in range(SLOT_BLOCK.bit_length() - 2, -1, -1))
SEG_TAB = 3 * N_EXPERTS

NT_DIMS = (((1,), (1,)), ((), ()))
TN_DIMS = (((0,), (0,)), ((), ()))


def _vmem_limit(*nbytes):
    est = int(sum(nbytes) * 1.2) + (4 << 20)
    return min(max(est, 16 << 20), VMEM_CAP)


def _params(vmem, n_grid):
    return pltpu.CompilerParams(dimension_semantics=("arbitrary",) * n_grid,
                                vmem_limit_bytes=vmem)


def _layer_norm_rows(y, g, b):
    mu = jnp.mean(y, axis=-1, keepdims=True)
    yc = y - mu
    var = jnp.mean(yc * yc, axis=-1, keepdims=True)
    return yc * lax.rsqrt(var + LN_EPS) * g + b


def _proj_kernel(x_ref, w_ref, b_ref, o_ref, *, n_chunk):
    xb = x_ref[...].astype(BF16)
    n_out = o_ref.shape[1]
    for c in range(0, n_out, n_chunk):
        acc = jnp.dot(xb, w_ref[:, c:c + n_chunk], preferred_element_type=F32)
        o_ref[:, c:c + n_chunk] = (acc + b_ref[:, c:c + n_chunk]).astype(o_ref.dtype)


def _proj(x, w_bf16, bias, n_chunk):
    t, d = x.shape
    n = w_bf16.shape[1]
    tm = ROW_TILE
    vmem = _vmem_limit(2 * tm * d * 4, 2 * d * n * 2, 2 * tm * n * 2, tm * n_chunk * 4 * 2, tm * d * 2)
    return pl.pallas_call(
        functools.partial(_proj_kernel, n_chunk=n_chunk),
        grid=(t // tm,),
        in_specs=[pl.BlockSpec((tm, d), lambda i: (i, 0)),
                  pl.BlockSpec((d, n), lambda i: (0, 0)),
                  pl.BlockSpec((1, n), lambda i: (0, 0))],
        out_specs=pl.BlockSpec((tm, n), lambda i: (i, 0)),
        out_shape=jax.ShapeDtypeStruct((t, n), BF16),
        compiler_params=_params(vmem, 1),
    )(x, w_bf16, bias)


def _ret_gammas():
    return [1.0 - 2.0 ** (-5.0 - h) for h in range(RET_HEADS)]


def _ret_kernel(q_ref, k_ref, v_ref, g_ref, dm_ref, qd_ref, kd_ref, o_ref, state_ref):
    @pl.when(pl.program_id(1) == 0)
    def _():
        state_ref[...] = jnp.zeros_like(state_ref)

    gammas = _ret_gammas()
    heads = range(RET_HEADS)
    qs = [q_ref[:, h * RET_QK_DIM:(h + 1) * RET_QK_DIM] for h in heads]
    ks = [k_ref[:, h * RET_QK_DIM:(h + 1) * RET_QK_DIM] for h in heads]
    vs = [v_ref[:, h * RET_V_DIM:(h + 1) * RET_V_DIM] for h in heads]
    scores = [(lax.dot_general(qs[h], ks[h], NT_DIMS, preferred_element_type=F32) * dm_ref[h]).astype(BF16)
              for h in heads]
    inter = [jnp.dot(qs[h], state_ref[h].astype(BF16), preferred_element_type=F32) * qd_ref[h] for h in heads]
    outs = [jnp.dot(scores[h], vs[h], preferred_element_type=F32) + inter[h] for h in heads]
    for h in heads:
        kdec = (ks[h].astype(F32) * kd_ref[h]).astype(BF16)
        upd = lax.dot_general(kdec, vs[h], TN_DIMS, preferred_element_type=F32)
        state_ref[h] = state_ref[h] * (gammas[h] ** q_ref.shape[0]) + upd
    for h in heads:
        o = outs[h]
        mu = jnp.mean(o, axis=-1, keepdims=True)
        oc = o - mu
        var = jnp.mean(oc * oc, axis=-1, keepdims=True)
        on = oc * lax.rsqrt(var + GN_EPS)
        gh = g_ref[:, h * RET_V_DIM:(h + 1) * RET_V_DIM].astype(F32)
        gate = gh / (1.0 + jnp.exp(-gh))
        o_ref[:, h * RET_V_DIM:(h + 1) * RET_V_DIM] = (gate * on).astype(o_ref.dtype)


def _retention_core(proj, batch, seq):
    c = RET_KERNEL_CHUNK
    nc = seq // c
    log_g = jnp.log(jnp.asarray(_ret_gammas(), F32))
    pos = jnp.arange(c, dtype=F32)
    rel = pos[:, None] - pos[None, :]
    scale = RET_QK_DIM ** -0.5
    dm = jnp.where(rel >= 0, jnp.exp(log_g[:, None, None] * jnp.maximum(rel, 0.0)), 0.0) * scale
    qd = jnp.exp(log_g[:, None] * (pos + 1.0))[..., None]
    kd = jnp.exp(log_g[:, None] * (c - 1.0 - pos))[..., None] * scale
    t = batch * seq
    row = lambda b, i: b * nc + i
    const3 = lambda b, i: (0, 0, 0)
    vmem = _vmem_limit(2 * c * RET_IN * 2, 2 * c * RET_V_ALL * 2, RET_HEADS * RET_QK_DIM * RET_V_DIM * 4 * 2,
                       8 << 20)
    return pl.pallas_call(
        _ret_kernel,
        grid=(batch, nc),
        in_specs=[pl.BlockSpec((c, RET_QK_ALL), lambda b, i: (row(b, i), 0)),
                  pl.BlockSpec((c, RET_QK_ALL), lambda b, i: (row(b, i), 1)),
                  pl.BlockSpec((c, RET_V_ALL), lambda b, i: (row(b, i), 1)),
                  pl.BlockSpec((c, RET_V_ALL), lambda b, i: (row(b, i), 2)),
                  pl.BlockSpec((RET_HEADS, c, c), const3),
                  pl.BlockSpec((RET_HEADS, c, 1), const3),
                  pl.BlockSpec((RET_HEADS, c, 1), const3)],
        out_specs=pl.BlockSpec((c, RET_V_ALL), lambda b, i: (row(b, i), 0)),
        out_shape=jax.ShapeDtypeStruct((t, RET_V_ALL), BF16),
        scratch_shapes=[pltpu.VMEM((RET_HEADS, RET_QK_DIM, RET_V_DIM), F32)],
        compiler_params=_params(vmem, 2),
    )(proj, proj, proj, proj, dm, qd, kd)


def _swa_kernel(sink_ref, bias_ref, q_ref, kvp_ref, kvc_ref, o_ref):
    c = SWA_BLOCK
    hd = SWA_HEAD_DIM
    qi = lax.broadcasted_iota(I32, (c, c), 0)
    kj = lax.broadcasted_iota(I32, (c, c), 1)
    cur = kj <= qi
    first_head = lax.broadcasted_iota(I32, (c, 2 * hd), 1) < hd
    ones = jnp.ones((2 * c, hd), BF16)
    zeros = jnp.zeros((2 * c, hd), BF16)
    scores, v_pairs = [], []
    for j in range(SWA_KV_HEADS):
        kcat = jnp.concatenate([kvp_ref[:, j * hd:(j + 1) * hd], kvc_ref[:, j * hd:(j + 1) * hd]], axis=0)
        k_t = kcat.astype(F32).T.astype(BF16)
        v0 = SWA_KV_HEADS * hd + j * hd
        vcat = jnp.concatenate([kvp_ref[:, v0:v0 + hd], kvc_ref[:, v0:v0 + hd]], axis=0)
        v_pairs.append(jnp.concatenate([jnp.concatenate([vcat, zeros, ones, zeros], axis=1),
                                        jnp.concatenate([zeros, vcat, zeros, ones], axis=1)], axis=0))
        for g in range(SWA_GROUP):
            h = j * SWA_GROUP + g
            qh = q_ref[:, h * hd:(h + 1) * hd] * (hd ** -0.5)
            s2 = jnp.dot(qh, k_t, preferred_element_type=F32)
            scores.append(jnp.where(cur, s2[:, c:], s2[:, :c]) + bias_ref[h])

    p2s, tails = [], []
    for h in range(SWA_Q_HEADS):
        sink = sink_ref[h]
        m = jnp.maximum(jnp.max(scores[h], axis=-1, keepdims=True), sink)
        p = jnp.exp(scores[h] - m)
        tails.append(jnp.exp(sink - m))
        p2s.append(jnp.concatenate([jnp.where(cur, 0.0, p), jnp.where(cur, p, 0.0)], axis=1).astype(BF16))

    outs = []
    for h in range(0, SWA_Q_HEADS, 2):
        ov = jnp.dot(jnp.concatenate(p2s[h:h + 2], axis=1), v_pairs[h // SWA_GROUP],
                     preferred_element_type=F32)
        den = ov[:, 2 * hd:] + jnp.where(first_head, tails[h], tails[h + 1])
        outs.append(ov[:, :2 * hd] / den)
    o_ref[...] = jnp.concatenate(outs, axis=-1).astype(o_ref.dtype)


def _swa_core(proj, sinks, batch, seq):
    c = SWA_BLOCK
    nb = seq // c
    t = batch * seq
    kv_col = SWA_Q_ALL // SWA_KV_ALL
    assert WINDOW == c
    qi = jnp.arange(c)[:, None]
    kj = jnp.arange(c)[None, :]
    cur = kj <= qi
    dist = jnp.where(cur, qi - kj, qi - kj + c).astype(F32)
    visible = jnp.stack([cur, jnp.ones_like(cur)])
    slopes = 2.0 ** (-8.0 * jnp.arange(1, SWA_Q_HEADS + 1, dtype=F32) / SWA_Q_HEADS)
    bias = jnp.where(visible[:, None], -slopes[None, :, None, None] * dist[None, None], -jnp.inf)
    vmem = _vmem_limit(2 * c * SWA_IN * 2 * 2, 2 * SWA_Q_HEADS * c * c * 4, 16 << 20)
    return pl.pallas_call(
        _swa_kernel,
        grid=(batch, nb),
        in_specs=[pl.BlockSpec(memory_space=pltpu.SMEM),
                  pl.BlockSpec((None, SWA_Q_HEADS, c, c), lambda b, n: (jnp.minimum(n, 1), 0, 0, 0)),
                  pl.BlockSpec((c, SWA_Q_ALL), lambda b, n: (b * nb + n, 0)),
                  pl.BlockSpec((c, SWA_KV_ALL), lambda b, n: (b * nb + jnp.maximum(n - 1, 0), kv_col)),
                  pl.BlockSpec((c, SWA_KV_ALL), lambda b, n: (b * nb + n, kv_col))],
        out_specs=pl.BlockSpec((c, SWA_Q_ALL), lambda b, n: (b * nb + n, 0)),
        out_shape=jax.ShapeDtypeStruct((t, SWA_Q_ALL), BF16),
        compiler_params=_params(vmem, 2),
    )(sinks, bias.astype(F32), proj, proj, proj)


def _outproj_ln_kernel(a_ref, w_ref, b_ref, x_ref, g_ref, beta_ref, o_ref):
    half = a_ref.shape[0] // 2
    for r in (0, half):
        m = jnp.dot(a_ref[r:r + half, :], w_ref[...], preferred_element_type=F32) + b_ref[...]
        o_ref[r:r + half, :] = _layer_norm_rows(DEEPNORM_ALPHA * x_ref[r:r + half, :] + m, g_ref[...],
                                                beta_ref[...])


def _outproj_ln(a, w_bf16, bias, x, g, beta):
    t, kin = a.shape
    d = D_MODEL
    tm = ROW_TILE
    vec = pl.BlockSpec((1, d), lambda i: (0, 0))
    vmem = _vmem_limit(2 * tm * kin * 2, 2 * kin * d * 2, 4 * tm * d * 4, 4 * tm * d * 4)
    return pl.pallas_call(
        _outproj_ln_kernel,
        grid=(t // tm,),
        in_specs=[pl.BlockSpec((tm, kin), lambda i: (i, 0)),
                  pl.BlockSpec((kin, d), lambda i: (0, 0)),
                  vec,
                  pl.BlockSpec((tm, d), lambda i: (i, 0)),
                  vec, vec],
        out_specs=pl.BlockSpec((tm, d), lambda i: (i, 0)),
        out_shape=jax.ShapeDtypeStruct((t, d), F32),
        compiler_params=_params(vmem, 1),
    )(a, w_bf16, bias, x, g, beta)


def _router_kernel(x_ref, wt_ref, b_ref, pos_ref, gate_ref, nbe_ref, base_ref, cnt_ref, carry_ref):
    @pl.when(pl.program_id(0) == 0)
    def _():
        carry_ref[...] = jnp.zeros_like(carry_ref)

    tb = TOK_BLOCK
    wt = wt_ref[...]
    wh = wt.astype(BF16)
    wl = (wt - wh.astype(F32)).astype(BF16)
    nt = x_ref.shape[0]
    x = x_ref[...]
    xh = x.astype(BF16)
    xl = (x - xh.astype(F32)).astype(BF16)
    logits = (lax.dot_general(wh, xh, NT_DIMS, preferred_element_type=F32)
              + lax.dot_general(wh, xl, NT_DIMS, preferred_element_type=F32)
              + lax.dot_general(wl, xh, NT_DIMS, preferred_element_type=F32)
              + b_ref[...])
    eidx = lax.broadcasted_iota(I32, (N_EXPERTS, nt), 0).astype(F32)
    work = logits
    sels, vals = [], []
    for _ in range(TOP_K):
        m = jnp.max(work, axis=0, keepdims=True)
        idx = jnp.min(jnp.where(work == m, eidx, float(N_EXPERTS)), axis=0, keepdims=True)
        sel = eidx == idx
        sels.append(sel)
        vals.append(m)
        work = jnp.where(sel, -jnp.inf, work)

    exps = [jnp.exp(v - vals[0]) for v in vals]
    den = exps[0] + exps[1] + exps[2] + exps[3]
    gates = jnp.concatenate([e / den for e in exps], axis=0)

    mask = jnp.zeros((N_EXPERTS, nt), F32)
    for sel in sels:
        mask = mask + jnp.where(sel, 1.0, 0.0)

    ti = lax.broadcasted_iota(I32, (tb, tb), 0)
    tj = lax.broadcasted_iota(I32, (tb, tb), 1)
    upper = jnp.where(ti < tj, 1.0, 0.0).astype(BF16)
    ei = lax.broadcasted_iota(I32, (N_EXPERTS, N_EXPERTS), 0)
    ej = lax.broadcasted_iota(I32, (N_EXPERTS, N_EXPERTS), 1)
    lower = jnp.where(ej < ei, 1.0, 0.0).astype(BF16)
    carry = carry_ref[...]

    for u in range(ROUTER_BLOCKS):
        blk = slice(u * tb, (u + 1) * tb)
        mask_u = mask[:, blk]
        rank_loc = jnp.dot(mask_u.astype(BF16), upper, preferred_element_type=F32)
        n_be = jnp.broadcast_to(jnp.sum(mask_u, axis=1, keepdims=True), (N_EXPERTS, LANES))
        off_be = jnp.dot(lower, n_be.astype(BF16), preferred_element_type=F32)
        pos_all = rank_loc + off_be[:, :1]
        poss = [jnp.sum(jnp.where(sel[:, blk], pos_all, 0.0), axis=0, keepdims=True) for sel in sels]

        pos_ref[u] = jnp.concatenate(poss, axis=0).astype(I32)
        gate_ref[u] = gates[:, blk]
        nbe_ref[u] = n_be
        base_ref[u] = carry
        carry = carry + n_be

    carry_ref[...] = carry
    cnt_ref[...] = carry


def _router(x, w_router_t, b_router_col):
    t, d = x.shape
    tb = TOK_BLOCK
    rb = ROUTER_BLOCKS
    assert tb <= 256, "per-block expert counts must stay exact in bf16"
    nblk = t // tb
    blk3 = pl.BlockSpec((rb, TOP_K, tb), lambda i: (i, 0, 0))
    seg3 = pl.BlockSpec((rb, N_EXPERTS, LANES), lambda i: (i, 0, 0))
    vmem = _vmem_limit(2 * rb * tb * d * 4, 2 * N_EXPERTS * d * 4, 8 << 20)
    return pl.pallas_call(
        _router_kernel,
        grid=(nblk // rb,),
        in_specs=[pl.BlockSpec((rb * tb, d), lambda i: (i, 0)),
                  pl.BlockSpec((N_EXPERTS, d), lambda i: (0, 0)),
                  pl.BlockSpec((N_EXPERTS, 1), lambda i: (0, 0))],
        out_specs=[blk3, blk3, seg3, seg3, pl.BlockSpec((N_EXPERTS, LANES), lambda i: (0, 0))],
        out_shape=[jax.ShapeDtypeStruct((nblk, TOP_K, tb), I32),
                   jax.ShapeDtypeStruct((nblk, TOP_K, tb), F32),
                   jax.ShapeDtypeStruct((nblk, N_EXPERTS, LANES), F32),
                   jax.ShapeDtypeStruct((nblk, N_EXPERTS, LANES), F32),
                   jax.ShapeDtypeStruct((N_EXPERTS, LANES), F32)],
        scratch_shapes=[pltpu.VMEM((N_EXPERTS, LANES), F32)],
        compiler_params=_params(vmem, 1),
    )(x, w_router_t, b_router_col)


def _rows(ref, row, n):
    return ref.at[pl.ds(pl.multiple_of(row * ROW_TILES, ROW_TILES), n * ROW_TILES)]


def _to_row_tiled(ref, value):
    n = value.shape[0]
    for c in range(ROW_TILES):
        ref[pl.ds(c, n, stride=ROW_TILES), :] = value[:, c * LANES:(c + 1) * LANES]


def _from_row_tiled(ref, n):
    return jnp.concatenate([ref[pl.ds(c, n, stride=ROW_TILES), :] for c in range(ROW_TILES)], axis=1)


def _run_copies(n, src_ref, src_row, dst_ref, dst_row, sem, bits, act, advance_src=True):
    for bit in bits:
        @pl.when((n & bit) != 0)
        def _():
            act(pltpu.make_async_copy(_rows(src_ref, src_row, bit), _rows(dst_ref, dst_row, bit), sem))
        if advance_src:
            src_row = src_row + (n & bit)
        dst_row = dst_row + (n & bit)


def _start(cp):
    cp.start()


def _wait(cp):
    cp.wait()


def _segment_loop(tab_ref, blk, act_on_run):
    base = blk * SEG_TAB

    def body(e, carry):
        act_on_run(tab_ref[base + e], tab_ref[base + N_EXPERTS + e], tab_ref[base + 2 * N_EXPERTS + e])
        return carry

    lax.fori_loop(0, N_EXPERTS, body, 0)


def _dispatch_kernel(tab_ref, fill_ref, pos_ref, x_ref, xs_hbm, sbuf, zeros_ref, sems, fill_sem, *, nblk):
    i = pl.program_id(0)
    par = i % 2
    tb = x_ref.shape[0]
    zrows = zeros_ref.shape[0] // ROW_TILES

    @pl.when(i == 0)
    def _():
        zeros_ref[...] = jnp.zeros_like(zeros_ref)

        def fill(act):
            def body(e, carry):
                _run_copies(fill_ref[1, e], zeros_ref, 0, xs_hbm, fill_ref[0, e], fill_sem, FILL_BITS, act,
                            advance_src=False)
                return carry
            lax.fori_loop(0, N_EXPERTS, body, 0)

        fill(_start)
        fill(_wait)

        first_tail = fill_ref[0, N_EXPERTS] // zrows
        n_chunks = xs_hbm.shape[0] // (zrows * ROW_TILES)

        def tail(act):
            def body(j, carry):
                act(pltpu.make_async_copy(zeros_ref, _rows(xs_hbm, j * zrows, zrows), fill_sem))
                return carry
            lax.fori_loop(first_tail, n_chunks, body, 0)

        tail(_start)
        tail(_wait)

    def drain(slot):
        pltpu.make_async_copy(sbuf.at[slot], _rows(xs_hbm, 0, SORT_ROWS), sems.at[slot]).wait()

    @pl.when(i >= 2)
    def _():
        drain(par)

    pos = pos_ref[0]
    pi = lax.broadcasted_iota(I32, (SORT_ROWS, tb), 0)
    onehot = jnp.where(pi == pos[0:1, :], 1.0, 0.0)
    for k in range(1, TOP_K):
        onehot = onehot + jnp.where(pi == pos[k:k + 1, :], 1.0, 0.0)
    xsorted = jnp.dot(onehot.astype(BF16), x_ref[...].astype(BF16), preferred_element_type=F32)
    buf = sbuf.at[par]
    _to_row_tiled(buf, xsorted)

    _segment_loop(tab_ref, i, lambda n, off, slot: _run_copies(n, buf, off, xs_hbm, slot, sems.at[par],
                                                               SEG_BITS, _start))

    @pl.when(i == nblk - 1)
    def _():
        drain(par)
        if nblk >= 2:
            drain(1 - par)


def _dispatch(tab, fill, pos, x, n_slots):
    nblk, _, tb = pos.shape
    d = x.shape[1]
    zrows = SLOT_BLOCK // 2
    vmem = _vmem_limit(2 * tb * d * 4, 2 * SORT_ROWS * d * 4, zrows * d * 4, 3 * SORT_ROWS * d * 4)
    grid_spec = pltpu.PrefetchScalarGridSpec(
        num_scalar_prefetch=2,
        grid=(nblk,),
        in_specs=[pl.BlockSpec((1, TOP_K, tb), lambda i, tab, fill: (i, 0, 0)),
                  pl.BlockSpec((tb, d), lambda i, tab, fill: (i, 0))],
        out_specs=pl.BlockSpec(memory_space=pl.ANY),
        scratch_shapes=[pltpu.VMEM((2, SORT_ROWS * ROW_TILES, LANES), F32),
                        pltpu.VMEM((zrows * ROW_TILES, LANES), F32),
                        pltpu.SemaphoreType.DMA((2,)),
                        pltpu.SemaphoreType.DMA(())],
    )
    return pl.pallas_call(
        functools.partial(_dispatch_kernel, nblk=nblk),
        grid_spec=grid_spec,
        out_shape=jax.ShapeDtypeStruct((n_slots * ROW_TILES, LANES), F32),
        compiler_params=pltpu.CompilerParams(dimension_semantics=("arbitrary",), has_side_effects=True,
                                             vmem_limit_bytes=vmem),
    )(tab, fill, pos, x)


def _expert_kernel(be_ref, rows_ref, nxt_ref, par_ref, xs_ref, w1_hbm, b1_ref, w2_hbm, b2_ref, y_ref,
                   w1f_ref, w2f_ref, w1b_ref, w2b_ref, sems, *, layer):
    i = pl.program_id(0)
    rows = rows_ref[i]
    used = rows > 0
    half = SLOT_BLOCK // 2
    slot = par_ref[i]

    def weight_copies(e, s):
        return (pltpu.make_async_copy(w1_hbm.at[layer, e], w1f_ref.at[s], sems.at[0, s]),
                pltpu.make_async_copy(w2_hbm.at[layer, e], w2f_ref.at[s], sems.at[1, s]))

    @pl.when(used & ((i == 0) | (be_ref[i] != be_ref[jnp.maximum(i - 1, 0)])))
    def _():
        @pl.when(i == 0)
        def _():
            for cp in weight_copies(be_ref[i], slot):
                cp.start()

        @pl.when(nxt_ref[i] >= 0)
        def _():
            for cp in weight_copies(nxt_ref[i], 1 - slot):
                cp.start()

        for cp in weight_copies(be_ref[i], slot):
            cp.wait()
        w1b_ref[...] = w1f_ref[slot].astype(BF16)
        w2b_ref[...] = w2f_ref[slot].astype(BF16)

    def part(ref, p):
        return ref.at[pl.ds(p * half * ROW_TILES, half * ROW_TILES)]

    def ffn(p):
        xb = _from_row_tiled(part(xs_ref, p), half).astype(BF16)
        h = jnp.dot(xb, w1b_ref[...], preferred_element_type=F32) + b1_ref[0]
        glu = jnp.minimum(h[:, :D_FF], SWIGLU_LIMIT)
        lin = jnp.clip(h[:, D_FF:], -SWIGLU_LIMIT, SWIGLU_LIMIT)
        act = glu / (1.0 + jnp.exp(-SWIGLU_ALPHA * glu)) * (lin + 1.0)
        _to_row_tiled(part(y_ref, p), jnp.dot(act.astype(BF16), w2b_ref[...], preferred_element_type=F32)
                      + b2_ref[0])

    def clear(p):
        part(y_ref, p)[...] = jnp.zeros((half * ROW_TILES, LANES), F32)

    pl.when(used)(lambda: ffn(0))
    pl.when(rows > half)(lambda: ffn(1))
    pl.when(jnp.logical_not(used))(lambda: clear(0))
    pl.when(rows <= half)(lambda: clear(1))


def _experts(layer, block_expert, block_rows, next_expert, weight_slot, xs, w1_all, b1_all, w2_all, b2_all):
    n_slots = xs.shape[0] // ROW_TILES
    bs = SLOT_BLOCK
    d = D_MODEL
    w_elems = d * 2 * D_FF + D_FF * d
    vmem = _vmem_limit(2 * bs * d * 4, 2 * w_elems * 4, w_elems * 2, 2 * bs * d * 4, 2 * bs * 2 * D_FF * 4)
    bmap = lambda i, be, rows, nxt, par: (layer, be[i], 0, 0)
    slot_blk = pl.BlockSpec((bs * ROW_TILES, LANES), lambda i, be, rows, nxt, par: (i, 0))
    any_spec = pl.BlockSpec(memory_space=pl.ANY)
    grid_spec = pltpu.PrefetchScalarGridSpec(
        num_scalar_prefetch=4,
        grid=(n_slots // bs,),
        in_specs=[slot_blk,
                  any_spec,
                  pl.BlockSpec((None, 1, 1, 2 * D_FF), bmap),
                  any_spec,
                  pl.BlockSpec((None, 1, 1, d), bmap)],
        out_specs=slot_blk,
        scratch_shapes=[pltpu.VMEM((2, d, 2 * D_FF), F32), pltpu.VMEM((2, D_FF, d), F32),
                        pltpu.VMEM((d, 2 * D_FF), BF16), pltpu.VMEM((D_FF, d), BF16),
                        pltpu.SemaphoreType.DMA((2, 2))],
    )
    return pl.pallas_call(
        functools.partial(_expert_kernel, layer=layer),
        grid_spec=grid_spec,
        out_shape=jax.ShapeDtypeStruct((n_slots * ROW_TILES, LANES), F32),
        compiler_params=_params(vmem, 1),
    )(block_expert, block_rows, next_expert, weight_slot, xs, w1_all, b1_all, w2_all, b2_all)


def _combine_kernel(tab_ref, y_hbm, pos_ref, gates_ref, x_ref, g_ref, beta_ref, o_ref, ybuf, sems, *, nblk):
    i = pl.program_id(0)
    par = i % 2
    tb = x_ref.shape[0]

    def fetch(blk, slot):
        dst = ybuf.at[slot]
        _segment_loop(tab_ref, blk, lambda n, off, first: _run_copies(n, y_hbm, first, dst, off, sems.at[slot],
                                                                      SEG_BITS, _start))

    @pl.when(i == 0)
    def _():
        fetch(0, 0)

    @pl.when(i + 1 < nblk)
    def _():
        fetch(i + 1, 1 - par)

    pltpu.make_async_copy(_rows(y_hbm, 0, SORT_ROWS), ybuf.at[par], sems.at[par]).wait()
    ysorted = _from_row_tiled(ybuf.at[par], SORT_ROWS)

    pos = pos_ref[...]
    gates = gates_ref[...]
    pj = lax.broadcasted_iota(I32, (tb, SORT_ROWS), 1)
    wsel = jnp.where(pj == pos[:, 0:1], gates[:, 0:1], 0.0)
    for k in range(1, TOP_K):
        wsel = wsel + jnp.where(pj == pos[:, k:k + 1], gates[:, k:k + 1], 0.0)
    f = jnp.dot(wsel.astype(BF16), ysorted.astype(BF16), preferred_element_type=F32)
    o_ref[...] = _layer_norm_rows(DEEPNORM_ALPHA * x_ref[...] + f, g_ref[...], beta_ref[...])


def _combine_ln(tab, y, pos_tok, gates_tok, x, g, beta):
    t, d = x.shape
    tb = TOK_BLOCK
    nblk = t // tb
    vec = pl.BlockSpec((1, d), lambda i, tab: (0, 0))
    tok4 = pl.BlockSpec((tb, TOP_K), lambda i, tab: (i, 0))
    vmem = _vmem_limit(2 * SORT_ROWS * d * 4, 4 * tb * d * 4, 3 * SORT_ROWS * d * 4, 2 * tb * SORT_ROWS * 4)
    grid_spec = pltpu.PrefetchScalarGridSpec(
        num_scalar_prefetch=1,
        grid=(nblk,),
        in_specs=[pl.BlockSpec(memory_space=pl.ANY), tok4, tok4,
                  pl.BlockSpec((tb, d), lambda i, tab: (i, 0)),
                  vec, vec],
        out_specs=pl.BlockSpec((tb, d), lambda i, tab: (i, 0)),
        scratch_shapes=[pltpu.VMEM((2, SORT_ROWS * ROW_TILES, LANES), F32),
                        pltpu.SemaphoreType.DMA((2,))],
    )
    return pl.pallas_call(
        functools.partial(_combine_kernel, nblk=nblk),
        grid_spec=grid_spec,
        out_shape=jax.ShapeDtypeStruct((t, d), F32),
        compiler_params=_params(vmem, 1),
    )(tab, y, pos_tok, gates_tok, x, g, beta)


def _moe_ln(layer, x, w_router, b_router, w1_all, b1_all, w2_all, b2_all, g, beta):
    t, d = x.shape
    bs = SLOT_BLOCK
    n_blocks = (t * TOP_K) // bs + N_EXPERTS
    n_slots = n_blocks * bs

    pos, gates, nbe, base, cnt = _router(x, w_router.T, b_router.reshape(N_EXPERTS, 1))
    counts = cnt[:, 0].astype(I32)
    padded = (counts + bs - 1) // bs * bs
    padded_end = jnp.cumsum(padded)
    start_padded = (padded_end - padded).astype(I32)
    block_start = jnp.arange(n_blocks, dtype=I32) * bs
    block_expert = jnp.minimum(jnp.sum(padded_end[None, :] <= block_start[:, None], axis=1),
                               N_EXPERTS - 1).astype(I32)
    block_rows = jnp.clip(counts[block_expert] - (block_start - start_padded[block_expert]), 0, bs).astype(I32)
    has_tokens = counts > 0
    e_ids = jnp.arange(N_EXPERTS, dtype=I32)
    later = jnp.where(has_tokens[None, :] & (e_ids[None, :] > e_ids[:, None]), e_ids[None, :], N_EXPERTS)
    next_e = jnp.min(later, axis=1)
    next_e = jnp.where(next_e == N_EXPERTS, -1, next_e).astype(I32)
    parity = ((jnp.cumsum(has_tokens.astype(I32)) - has_tokens.astype(I32)) % 2).astype(I32)
    next_expert = next_e[block_expert]
    weight_slot = parity[block_expert]

    run_len = nbe[:, :, 0].astype(I32)
    run_off = jnp.cumsum(run_len, axis=1) - run_len
    run_slot = start_padded[None, :] + base[:, :, 0].astype(I32)
    tab = jnp.concatenate([run_len, run_off, run_slot], axis=1).reshape(-1)
    zero1 = jnp.zeros((1,), I32)
    fill = jnp.stack([jnp.concatenate([start_padded + counts, padded_end[-1:]]),
                      jnp.concatenate([padded - counts, zero1])]).astype(I32)

    xs = _dispatch(tab, fill, pos, x, n_slots)
    y = _experts(layer, block_expert, block_rows, next_expert, weight_slot, xs, w1_all, b1_all, w2_all, b2_all)
    pos_tok = pos.transpose(0, 2, 1).reshape(t, TOP_K)
    gates_tok = gates.transpose(0, 2, 1).reshape(t, TOP_K)
    return _combine_ln(tab, y, pos_tok, gates_tok, x, g, beta)


def kernel(x, w_ret_in, w_ret_out, w_swa_in, b_swa_in, w_swa_out, b_swa_out, swa_sinks, ln_mix_g, ln_mix_b,
           w_router, b_router, w_exp_in, b_exp_in, w_exp_out, b_exp_out, ln_ffn_g, ln_ffn_b):
    batch, seq, d = x.shape
    t = batch * seq
    xt = x.reshape(t, d)
    row = lambda v: v.reshape(1, -1)
    zeros_d = jnp.zeros((1, d), F32)
    w1_all, w2_all = w_exp_in, w_exp_out
    b1_all = b_exp_in.reshape(DEPTH, N_EXPERTS, 1, 2 * D_FF)
    b2_all = b_exp_out.reshape(DEPTH, N_EXPERTS, 1, d)
    for i in range(DEPTH):
        j = i // 2
        if i % 2 == 0:
            proj = _proj(xt, w_ret_in[j].astype(BF16), jnp.zeros((1, RET_IN), F32), PROJ_N_CHUNK)
            mixed = _retention_core(proj, batch, seq)
            w_out, b_out = w_ret_out[j].astype(BF16), zeros_d
        else:
            proj = _proj(xt, w_swa_in[j].astype(BF16), row(b_swa_in[j]), SWA_IN)
            mixed = _swa_core(proj, swa_sinks[j], batch, seq)
            w_out, b_out = w_swa_out[j].astype(BF16), row(b_swa_out[j])
        xt = _outproj_ln(mixed, w_out, b_out, xt, row(ln_mix_g[i]), row(ln_mix_b[i]))
        xt = _moe_ln(i, xt, w_router[i], b_router[i], w1_all, b1_all, w2_all, b2_all,
                     row(ln_ffn_g[i]), row(ln_ffn_b[i]))
    return xt.reshape(batch, seq, d)
```

```python
import functools
import math

import jax
import jax.numpy as jnp
from jax import lax
from jax.experimental import pallas as pl
from jax.experimental.pallas import tpu as pltpu

F32 = jnp.float32
BF16 = jnp.bfloat16
I32 = jnp.int32
U32 = jnp.uint32

D_MODEL = 1024
DEPTH = 2
RET_HEADS = 4
RET_QK_DIM = 256
RET_V_DIM = 512
RET_CHUNK = 128
RET_QK_ALL = RET_HEADS * RET_QK_DIM
RET_V_ALL = RET_HEADS * RET_V_DIM
RET_IN = 2 * RET_QK_ALL + 2 * RET_V_ALL
SWA_Q_HEADS = 16
SWA_KV_HEADS = 2
SWA_GROUP = 8
SWA_HEAD_DIM = 64
SWA_BLOCK = 128
WINDOW = 128
SWA_Q_ALL = SWA_Q_HEADS * SWA_HEAD_DIM
SWA_KV_ALL = 2 * SWA_KV_HEADS * SWA_HEAD_DIM
SWA_IN = SWA_Q_ALL + SWA_KV_ALL
N_EXPERTS = 32
TOP_K = 4
D_FF = 1024
SWIGLU_ALPHA = 1.702
SWIGLU_LIMIT = 7.0
LN_EPS = 1e-5
GN_EPS = 1e-6
DEEPNORM_ALPHA = (2 * DEPTH) ** 0.25

V7X_VMEM_BYTES = 64 * 1024 * 1024
VMEM_CAP = V7X_VMEM_BYTES - 8 * 1024 * 1024
ROW_TILE = 512
RET_KERNEL_CHUNK = 256
PROJ_N_CHUNK = 1536
TOK_BLOCK = 256
ROUTER_BLOCKS = 4
SLOT_BLOCK = 512
SUBLANES = 8
LANES = 128
ROW_TILES = D_MODEL // LANES
assert ROW_TILES == SUBLANES
SORT_ROWS = TOP_K * TOK_BLOCK
SEG_BITS = tuple(1 << b for b in range(TOK_BLOCK.bit_length() - 1, -1, -1))
FILL_BITS = tuple(1 << b for b in range(SLOT_BLOCK.bit_length() - 2, -1, -1))
SEG_TAB = 3 * N_EXPERTS

NT_DIMS = (((1,), (1,)), ((), ()))
TN_DIMS = (((0,), (0,)), ((), ()))


def _vmem_limit(*nbytes):
    est = int(sum(nbytes) * 1.2) + (4 << 20)
    return min(max(est, 16 << 20), VMEM_CAP)


def _params(vmem, n_grid):
    return pltpu.CompilerParams(dimension_semantics=("arbitrary",) * n_grid,
                                vmem_limit_bytes=vmem)


def _layer_norm_rows(y, g, b):
    mu = jnp.mean(y, axis=-1, keepdims=True)
    yc = y - mu
    var = jnp.mean(yc * yc, axis=-1, keepdims=True)
    return yc * lax.rsqrt(var + LN_EPS) * g + b


def _proj_kernel(x_ref, w_ref, b_ref, o_ref, *, n_chunk):
    xb = x_ref[...].astype(BF16)
    n_out = o_ref.shape[1]
    for c in range(0, n_out, n_chunk):
        acc = jnp.dot(xb, w_ref[:, c:c + n_chunk], preferred_element_type=F32)
        o_ref[:, c:c + n_chunk] = (acc + b_ref[:, c:c + n_chunk]).astype(o_ref.dtype)


def _proj(x, w_bf16, bias, n_chunk):
    t, d = x.shape
    n = w_bf16.shape[1]
    tm = ROW_TILE
    vmem = _vmem_limit(2 * tm * d * 4, 2 * d * n * 2, 2 * tm * n * 2, tm * n_chunk * 4 * 2, tm * d * 2)
    return pl.pallas_call(
        functools.partial(_proj_kernel, n_chunk=n_chunk),
        grid=(t // tm,),
        in_specs=[pl.BlockSpec((tm, d), lambda i: (i, 0)),
                  pl.BlockSpec((d, n), lambda i: (0, 0)),
                  pl.BlockSpec((1, n), lambda i: (0, 0))],
        out_specs=pl.BlockSpec((tm, n), lambda i: (i, 0)),
        out_shape=jax.ShapeDtypeStruct((t, n), BF16),
        compiler_params=_params(vmem, 1),
    )(x, w_bf16, bias)


def _ret_gammas():
    return [1.0 - 2.0 ** (-5.0 - h) for h in range(RET_HEADS)]


def _ret_kernel(q_ref, k_ref, v_ref, g_ref, dm_ref, qd_ref, kd_ref, o_ref, state_ref):
    @pl.when(pl.program_id(1) == 0)
    def _():
        state_ref[...] = jnp.zeros_like(state_ref)

    gammas = _ret_gammas()
    heads = range(RET_HEADS)
    qs = [q_ref[:, h * RET_QK_DIM:(h + 1) * RET_QK_DIM] for h in heads]
    ks = [k_ref[:, h * RET_QK_DIM:(h + 1) * RET_QK_DIM] for h in heads]
    vs = [v_ref[:, h * RET_V_DIM:(h + 1) * RET_V_DIM] for h in heads]
    scores = [(lax.dot_general(qs[h], ks[h], NT_DIMS, preferred_element_type=F32) * dm_ref[h]).astype(BF16)
              for h in heads]
    inter = [jnp.dot(qs[h], state_ref[h].astype(BF16), preferred_element_type=F32) * qd_ref[h] for h in heads]
    outs = [jnp.dot(scores[h], vs[h], preferred_element_type=F32) + inter[h] for h in heads]
    for h in heads:
        kdec = (ks[h].astype(F32) * kd_ref[h]).astype(BF16)
        upd = lax.dot_general(kdec, vs[h], TN_DIMS, preferred_element_type=F32)
        state_ref[h] = state_ref[h] * (gammas[h] ** q_ref.shape[0]) + upd
    for h in heads:
        o = outs[h]
        mu = jnp.mean(o, axis=-1, keepdims=True)
        oc = o - mu
        var = jnp.mean(oc * oc, axis=-1, keepdims=True)
        on = oc * lax.rsqrt(var + GN_EPS)
        gh = g_ref[:, h * RET_V_DIM:(h + 1) * RET_V_DIM].astype(F32)
        gate = gh / (1.0 + jnp.exp(-gh))
        o_ref[:, h * RET_V_DIM:(h + 1) * RET_V_DIM] = (gate * on).astype(o_ref.dtype)


def _retention_core(proj, batch, seq):
    c = RET_KERNEL_CHUNK
    nc = seq // c
    log_g = jnp.log(jnp.asarray(_ret_gammas(), F32))
    pos = jnp.arange(c, dtype=F32)
    rel = pos[:, None] - pos[None, :]
    scale = RET_QK_DIM ** -0.5
    dm = jnp.where(rel >= 0, jnp.exp(log_g[:, None, None] * jnp.maximum(rel, 0.0)), 0.0) * scale
    qd = jnp.exp(log_g[:, None] * (pos + 1.0))[..., None]
    kd = jnp.exp(log_g[:, None] * (c - 1.0 - pos))[..., None] * scale
    t = batch * seq
    row = lambda b, i: b * nc + i
    const3 = lambda b, i: (0, 0, 0)
    vmem = _vmem_limit(2 * c * RET_IN * 2, 2 * c * RET_V_ALL * 2, RET_HEADS * RET_QK_DIM * RET_V_DIM * 4 * 2,
                       8 << 20)
    return pl.pallas_call(
        _ret_kernel,
        grid=(batch, nc),
        in_specs=[pl.BlockSpec((c, RET_QK_ALL), lambda b, i: (row(b, i), 0)),
                  pl.BlockSpec((c, RET_QK_ALL), lambda b, i: (row(b, i), 1)),
                  pl.BlockSpec((c, RET_V_ALL), lambda b, i: (row(b, i), 1)),
                  pl.BlockSpec((c, RET_V_ALL), lambda b, i: (row(b, i), 2)),
                  pl.BlockSpec((RET_HEADS, c, c), const3),
                  pl.BlockSpec((RET_HEADS, c, 1), const3),
                  pl.BlockSpec((RET_HEADS, c, 1), const3)],
        out_specs=pl.BlockSpec((c, RET_V_ALL), lambda b, i: (row(b, i), 0)),
        out_shape=jax.ShapeDtypeStruct((t, RET_V_ALL), BF16),
        scratch_shapes=[pltpu.VMEM((RET_HEADS, RET_QK_DIM, RET_V_DIM), F32)],
        compiler_params=_params(vmem, 2),
    )(proj, proj, proj, proj, dm, qd, kd)


def _swa_kernel(sink_ref, bias_ref, q_ref, kvp_ref, kvc_ref, o_ref):
    c = SWA_BLOCK
    hd = SWA_HEAD_DIM
    qi = lax.broadcasted_iota(I32, (c, c), 0)
    kj = lax.broadcasted_iota(I32, (c, c), 1)
    cur = kj <= qi
    first_head = lax.broadcasted_iota(I32, (c, 2 * hd), 1) < hd
    ones = jnp.ones((2 * c, hd), BF16)
    zeros = jnp.zeros((2 * c, hd), BF16)
    scores, v_pairs = [], []
    for j in range(SWA_KV_HEADS):
        kcat = jnp.concatenate([kvp_ref[:, j * hd:(j + 1) * hd], kvc_ref[:, j * hd:(j + 1) * hd]], axis=0)
        k_t = kcat.astype(F32).T.astype(BF16)
        v0 = SWA_KV_HEADS * hd + j * hd
        vcat = jnp.concatenate([kvp_ref[:, v0:v0 + hd], kvc_ref[:, v0:v0 + hd]], axis=0)
        v_pairs.append(jnp.concatenate([jnp.concatenate([vcat, zeros, ones, zeros], axis=1),
                                        jnp.concatenate([zeros, vcat, zeros, ones], axis=1)], axis=0))
        for g in range(SWA_GROUP):
            h = j * SWA_GROUP + g
            qh = q_ref[:, h * hd:(h + 1) * hd] * (hd ** -0.5)
            s2 = jnp.dot(qh, k_t, preferred_element_type=F32)
            scores.append(jnp.where(cur, s2[:, c:], s2[:, :c]) + bias_ref[h])

    p2s, tails = [], []
    for h in range(SWA_Q_HEADS):
        sink = sink_ref[h]
        m = jnp.maximum(jnp.max(scores[h], axis=-1, keepdims=True), sink)
        p = jnp.exp(scores[h] - m)
        tails.append(jnp.exp(sink - m))
        p2s.append(jnp.concatenate([jnp.where(cur, 0.0, p), jnp.where(cur, p, 0.0)], axis=1).astype(BF16))

    outs = []
    for h in range(0, SWA_Q_HEADS, 2):
        ov = jnp.dot(jnp.concatenate(p2s[h:h + 2], axis=1), v_pairs[h // SWA_GROUP],
                     preferred_element_type=F32)
        den = ov[:, 2 * hd:] + jnp.where(first_head, tails[h], tails[h + 1])
        outs.append(ov[:, :2 * hd] / den)
    o_ref[...] = jnp.concatenate(outs, axis=-1).astype(o_ref.dtype)


def _swa_core(proj, sinks, batch, seq):
    c = SWA_BLOCK
    nb = seq // c
    t = batch * seq
    kv_col = SWA_Q_ALL // SWA_KV_ALL
    assert WINDOW == c
    qi = jnp.arange(c)[:, None]
    kj = jnp.arange(c)[None, :]
    cur = kj <= qi
    dist = jnp.where(cur, qi - kj, qi - kj + c).astype(F32)
    visible = jnp.stack([cur, jnp.ones_like(cur)])
    slopes = 2.0 ** (-8.0 * jnp.arange(1, SWA_Q_HEADS + 1, dtype=F32) / SWA_Q_HEADS)
    bias = jnp.where(visible[:, None], -slopes[None, :, None, None] * dist[None, None], -jnp.inf)
    vmem = _vmem_limit(2 * c * SWA_IN * 2 * 2, 2 * SWA_Q_HEADS * c * c * 4, 16 << 20)
    return pl.pallas_call(
        _swa_kernel,
        grid=(batch, nb),
        in_specs=[pl.BlockSpec(memory_space=pltpu.SMEM),
                  pl.BlockSpec((None, SWA_Q_HEADS, c, c), lambda b, n: (jnp.minimum(n, 1), 0, 0, 0)),
                  pl.BlockSpec((c, SWA_Q_ALL), lambda b, n: (b * nb + n, 0)),
                  pl.BlockSpec((c, SWA_KV_ALL), lambda b, n: (b * nb + jnp.maximum(n - 1, 0), kv_col)),
                  pl.BlockSpec((c, SWA_KV_ALL), lambda b, n: (b * nb + n, kv_col))],
        out_specs=pl.BlockSpec((c, SWA_Q_ALL), lambda b, n: (b * nb + n, 0)),
        out_shape=jax.ShapeDtypeStruct((t, SWA_Q_ALL), BF16),
        compiler_params=_params(vmem, 2),
    )(sinks, bias.astype(F32), proj, proj, proj)


def _outproj_ln_kernel(a_ref, w_ref, b_ref, x_ref, g_ref, beta_ref, o_ref):
    half = a_ref.shape[0] // 2
    for r in (0, half):
        m = jnp.dot(a_ref[r:r + half, :], w_ref[...], preferred_element_type=F32) + b_ref[...]
        o_ref[r:r + half, :] = _layer_norm_rows(DEEPNORM_ALPHA * x_ref[r:r + half, :] + m, g_ref[...],
                                                beta_ref[...])


def _outproj_ln(a, w_bf16, bias, x, g, beta):
    t, kin = a.shape
    d = D_MODEL
    tm = ROW_TILE
    vec = pl.BlockSpec((1, d), lambda i: (0, 0))
    vmem = _vmem_limit(2 * tm * kin * 2, 2 * kin * d * 2, 4 * tm * d * 4, 4 * tm * d * 4)
    return pl.pallas_call(
        _outproj_ln_kernel,
        grid=(t // tm,),
        in_specs=[pl.BlockSpec((tm, kin), lambda i: (i, 0)),
                  pl.BlockSpec((kin, d), lambda i: (0, 0)),
                  vec,
                  pl.BlockSpec((tm, d), lambda i: (i, 0)),
                  vec, vec],
        out_specs=pl.BlockSpec((tm, d), lambda i: (i, 0)),
        out_shape=jax.ShapeDtypeStruct((t, d), F32),
        compiler_params=_params(vmem, 1),
    )(a, w_bf16, bias, x, g, beta)


def _router_kernel(x_ref, wt_ref, b_ref, pos_ref, gate_ref, nbe_ref, off_ref, base_ref, cnt_ref, carry_ref):
    @pl.when(pl.program_id(0) == 0)
    def _():
        carry_ref[...] = jnp.zeros_like(carry_ref)

    tb = TOK_BLOCK
    wt = wt_ref[...]
    wh = wt.astype(BF16)
    wl = (wt - wh.astype(F32)).astype(BF16)
    nt = x_ref.shape[0]
    x = x_ref[...]
    xh = x.astype(BF16)
    xl = (x - xh.astype(F32)).astype(BF16)
    logits = (lax.dot_general(wh, xh, NT_DIMS, preferred_element_type=F32)
              + lax.dot_general(wh, xl, NT_DIMS, preferred_element_type=F32)
              + lax.dot_general(wl, xh, NT_DIMS, preferred_element_type=F32)
              + b_ref[...])
    eidx = lax.broadcasted_iota(I32, (N_EXPERTS, nt), 0).astype(F32)
    work = logits
    sels, vals = [], []
    for _ in range(TOP_K):
        m = jnp.max(work, axis=0, keepdims=True)
        idx = jnp.min(jnp.where(work == m, eidx, float(N_EXPERTS)), axis=0, keepdims=True)
        sel = eidx == idx
        sels.append(sel)
        vals.append(m)
        work = jnp.where(sel, -jnp.inf, work)

    exps = [jnp.exp(v - vals[0]) for v in vals]
    den = exps[0] + exps[1] + exps[2] + exps[3]
    gates = jnp.concatenate([e / den for e in exps], axis=0)

    mask = jnp.zeros((N_EXPERTS, nt), F32)
    for sel in sels:
        mask = mask + jnp.where(sel, 1.0, 0.0)

    ti = lax.broadcasted_iota(I32, (tb, tb), 0)
    tj = lax.broadcasted_iota(I32, (tb, tb), 1)
    upper = jnp.where(ti < tj, 1.0, 0.0).astype(BF16)
    ei = lax.broadcasted_iota(I32, (N_EXPERTS, N_EXPERTS), 0)
    ej = lax.broadcasted_iota(I32, (N_EXPERTS, N_EXPERTS), 1)
    lower = jnp.where(ej < ei, 1.0, 0.0).astype(BF16)
    carry = carry_ref[...]

    for u in range(ROUTER_BLOCKS):
        blk = slice(u * tb, (u + 1) * tb)
        mask_u = mask[:, blk]
        rank_loc = jnp.dot(mask_u.astype(BF16), upper, preferred_element_type=F32)
        n_be = jnp.broadcast_to(jnp.sum(mask_u, axis=1, keepdims=True), (N_EXPERTS, LANES))
        off_be = jnp.dot(lower, n_be.astype(BF16), preferred_element_type=F32)
        pos_all = rank_loc + off_be[:, :1]
        poss = [jnp.sum(jnp.where(sel[:, blk], pos_all, 0.0), axis=0, keepdims=True) for sel in sels]

        pos_ref[u] = jnp.concatenate(poss, axis=0).astype(I32)
        gate_ref[u] = gates[:, blk]
        nbe_ref[u] = n_be
        off_ref[u] = off_be
        base_ref[u] = carry
        carry = carry + n_be

    carry_ref[...] = carry
    cnt_ref[...] = carry


def _router(x, w_router_t, b_router_col):
    t, d = x.shape
    tb = TOK_BLOCK
    rb = ROUTER_BLOCKS
    assert tb <= 256, "per-block expert counts must stay exact in bf16"
    nblk = t // tb
    blk3 = pl.BlockSpec((rb, TOP_K, tb), lambda i: (i, 0, 0))
    seg3 = pl.BlockSpec((rb, N_EXPERTS, LANES), lambda i: (i, 0, 0))
    vmem = _vmem_limit(2 * rb * tb * d * 4, 2 * N_EXPERTS * d * 4, 8 << 20)
    return pl.pallas_call(
        _router_kernel,
        grid=(nblk // rb,),
        in_specs=[pl.BlockSpec((rb * tb, d), lambda i: (i, 0)),
                  pl.BlockSpec((N_EXPERTS, d), lambda i: (0, 0)),
                  pl.BlockSpec((N_EXPERTS, 1), lambda i: (0, 0))],
        out_specs=[blk3, blk3, seg3, seg3, seg3, pl.BlockSpec((N_EXPERTS, LANES), lambda i: (0, 0))],
        out_shape=[jax.ShapeDtypeStruct((nblk, TOP_K, tb), I32),
                   jax.ShapeDtypeStruct((nblk, TOP_K, tb), F32),
                   jax.ShapeDtypeStruct((nblk, N_EXPERTS, LANES), F32),
                   jax.ShapeDtypeStruct((nblk, N_EXPERTS, LANES), F32),
                   jax.ShapeDtypeStruct((nblk, N_EXPERTS, LANES), F32),
                   jax.ShapeDtypeStruct((N_EXPERTS, LANES), F32)],
        scratch_shapes=[pltpu.VMEM((N_EXPERTS, LANES), F32)],
        compiler_params=_params(vmem, 1),
    )(x, w_router_t, b_router_col)


BLK_EXPERT, BLK_ROWS, BLK_NEXT, BLK_SLOT = range(4)
EXP_START, EXP_FIRST_PAD, EXP_NUM_PAD, EXP_SLOT = range(4)


def _plan_kernel(cnt_ref, blk_ref, exp_ref, *, n_blocks):
    bs = SLOT_BLOCK
    shift = bs.bit_length() - 1

    def init(i, carry):
        blk_ref[BLK_EXPERT, i] = N_EXPERTS - 1
        blk_ref[BLK_ROWS, i] = 0
        blk_ref[BLK_NEXT, i] = -1
        blk_ref[BLK_SLOT, i] = 0
        return carry

    lax.fori_loop(0, n_blocks, init, 0)

    def forward(e, carry):
        start, n_used = carry
        n = cnt_ref[e]
        padded = ((n + (bs - 1)) >> shift) << shift
        exp_ref[EXP_START, e] = start
        exp_ref[EXP_FIRST_PAD, e] = start + n
        exp_ref[EXP_NUM_PAD, e] = padded - n
        exp_ref[EXP_SLOT, e] = n_used & 1
        return start + padded, n_used + (n > 0).astype(I32)

    total, _ = lax.fori_loop(0, N_EXPERTS, forward, (jnp.int32(0), jnp.int32(0)))
    exp_ref[EXP_START, N_EXPERTS] = total
    exp_ref[EXP_FIRST_PAD, N_EXPERTS] = total
    exp_ref[EXP_NUM_PAD, N_EXPERTS] = 0
    exp_ref[EXP_SLOT, N_EXPERTS] = 0

    def backward(k, nxt):
        e = N_EXPERTS - 1 - k
        n = cnt_ref[e]
        first_blk = exp_ref[EXP_START, e] >> shift
        slot = exp_ref[EXP_SLOT, e]

        def per_block(j, carry):
            i = first_blk + j
            blk_ref[BLK_EXPERT, i] = e
            blk_ref[BLK_ROWS, i] = jnp.minimum(n - (j << shift), bs)
            blk_ref[BLK_NEXT, i] = nxt
            blk_ref[BLK_SLOT, i] = slot
            return carry

        lax.fori_loop(0, (n + (bs - 1)) >> shift, per_block, 0)
        return jnp.where(n > 0, e, nxt)

    lax.fori_loop(0, N_EXPERTS, backward, jnp.int32(-1))


def _plan(counts, n_blocks):
    assert SLOT_BLOCK & (SLOT_BLOCK - 1) == 0
    smem = pl.BlockSpec(memory_space=pltpu.SMEM)
    return pl.pallas_call(
        functools.partial(_plan_kernel, n_blocks=n_blocks),
        in_specs=[smem],
        out_specs=[smem, smem],
        out_shape=[jax.ShapeDtypeStruct((4, n_blocks), I32),
                   jax.ShapeDtypeStruct((4, N_EXPERTS + 1), I32)],
    )(counts)


def _rows(ref, row, n):
    return ref.at[pl.ds(pl.multiple_of(row * ROW_TILES, ROW_TILES), n * ROW_TILES)]


def _to_row_tiled(ref, value):
    n = value.shape[0]
    for c in range(ROW_TILES):
        ref[pl.ds(c, n, stride=ROW_TILES), :] = value[:, c * LANES:(c + 1) * LANES]


def _from_row_tiled(ref, n):
    return jnp.concatenate([ref[pl.ds(c, n, stride=ROW_TILES), :] for c in range(ROW_TILES)], axis=1)


def _run_copies(n, src_ref, src_row, dst_ref, dst_row, sem, bits, act, advance_src=True):
    for bit in bits:
        @pl.when((n & bit) != 0)
        def _():
            act(pltpu.make_async_copy(_rows(src_ref, src_row, bit), _rows(dst_ref, dst_row, bit), sem))
        if advance_src:
            src_row = src_row + (n & bit)
        dst_row = dst_row + (n & bit)


def _start(cp):
    cp.start()


def _wait(cp):
    cp.wait()


def _segment_loop(tab_ref, blk, act_on_run):
    base = blk * SEG_TAB

    def body(e, carry):
        act_on_run(tab_ref[base + e], tab_ref[base + N_EXPERTS + e], tab_ref[base + 2 * N_EXPERTS + e])
        return carry

    lax.fori_loop(0, N_EXPERTS, body, 0)


def _dispatch_kernel(tab_ref, fill_ref, pos_ref, x_ref, xs_hbm, sbuf, zeros_ref, sems, fill_sem, *, nblk):
    i = pl.program_id(0)
    par = i % 2
    tb = x_ref.shape[0]
    zrows = zeros_ref.shape[0] // ROW_TILES

    @pl.when(i == 0)
    def _():
        zeros_ref[...] = jnp.zeros_like(zeros_ref)

        def fill(act):
            def body(e, carry):
                _run_copies(fill_ref[EXP_NUM_PAD, e], zeros_ref, 0, xs_hbm, fill_ref[EXP_FIRST_PAD, e], fill_sem,
                            FILL_BITS, act, advance_src=False)
                return carry
            lax.fori_loop(0, N_EXPERTS, body, 0)

        fill(_start)
        fill(_wait)

        first_tail = fill_ref[EXP_START, N_EXPERTS] // zrows
        n_chunks = xs_hbm.shape[0] // (zrows * ROW_TILES)

        def tail(act):
            def body(j, carry):
                act(pltpu.make_async_copy(zeros_ref, _rows(xs_hbm, j * zrows, zrows), fill_sem))
                return carry
            lax.fori_loop(first_tail, n_chunks, body, 0)

        tail(_start)
        tail(_wait)

    def drain(slot):
        pltpu.make_async_copy(sbuf.at[slot], _rows(xs_hbm, 0, SORT_ROWS), sems.at[slot]).wait()

    @pl.when(i >= 2)
    def _():
        drain(par)

    pos = pos_ref[0]
    pi = lax.broadcasted_iota(I32, (SORT_ROWS, tb), 0)
    onehot = jnp.where(pi == pos[0:1, :], 1.0, 0.0)
    for k in range(1, TOP_K):
        onehot = onehot + jnp.where(pi == pos[k:k + 1, :], 1.0, 0.0)
    xsorted = jnp.dot(onehot.astype(BF16), x_ref[...].astype(BF16), preferred_element_type=F32)
    buf = sbuf.at[par]
    _to_row_tiled(buf, xsorted)

    _segment_loop(tab_ref, i, lambda n, off, slot: _run_copies(n, buf, off, xs_hbm, slot, sems.at[par],
                                                               SEG_BITS, _start))

    @pl.when(i == nblk - 1)
    def _():
        drain(par)
        if nblk >= 2:
            drain(1 - par)


def _dispatch(tab, fill, pos, x, n_slots):
    nblk, _, tb = pos.shape
    d = x.shape[1]
    zrows = SLOT_BLOCK // 2
    vmem = _vmem_limit(2 * tb * d * 4, 2 * SORT_ROWS * d * 4, zrows * d * 4, 3 * SORT_ROWS * d * 4)
    grid_spec = pltpu.PrefetchScalarGridSpec(
        num_scalar_prefetch=2,
        grid=(nblk,),
        in_specs=[pl.BlockSpec((1, TOP_K, tb), lambda i, tab, fill: (i, 0, 0)),
                  pl.BlockSpec((tb, d), lambda i, tab, fill: (i, 0))],
        out_specs=pl.BlockSpec(memory_space=pl.ANY),
        scratch_shapes=[pltpu.VMEM((2, SORT_ROWS * ROW_TILES, LANES), F32),
                        pltpu.VMEM((zrows * ROW_TILES, LANES), F32),
                        pltpu.SemaphoreType.DMA((2,)),
                        pltpu.SemaphoreType.DMA(())],
    )
    return pl.pallas_call(
        functools.partial(_dispatch_kernel, nblk=nblk),
        grid_spec=grid_spec,
        out_shape=jax.ShapeDtypeStruct((n_slots * ROW_TILES, LANES), F32),
        compiler_params=pltpu.CompilerParams(dimension_semantics=("arbitrary",), has_side_effects=True,
                                             vmem_limit_bytes=vmem),
    )(tab, fill, pos, x)


def _expert_kernel(blk_ref, xs_ref, w1_hbm, b1_ref, w2_hbm, b2_ref, y_ref,
                   w1f_ref, w2f_ref, w1b_ref, w2b_ref, sems, *, layer):
    i = pl.program_id(0)
    expert = blk_ref[BLK_EXPERT, i]
    rows = blk_ref[BLK_ROWS, i]
    nxt = blk_ref[BLK_NEXT, i]
    slot = blk_ref[BLK_SLOT, i]
    used = rows > 0
    half = SLOT_BLOCK // 2

    def weight_copies(e, s):
        return (pltpu.make_async_copy(w1_hbm.at[layer, e], w1f_ref.at[s], sems.at[0, s]),
                pltpu.make_async_copy(w2_hbm.at[layer, e], w2f_ref.at[s], sems.at[1, s]))

    @pl.when(used & ((i == 0) | (expert != blk_ref[BLK_EXPERT, jnp.maximum(i - 1, 0)])))
    def _():
        @pl.when(i == 0)
        def _():
            for cp in weight_copies(expert, slot):
                cp.start()

        @pl.when(nxt >= 0)
        def _():
            for cp in weight_copies(nxt, 1 - slot):
                cp.start()

        for cp in weight_copies(expert, slot):
            cp.wait()
        w1b_ref[...] = w1f_ref[slot].astype(BF16)
        w2b_ref[...] = w2f_ref[slot].astype(BF16)

    def part(ref, p):
        return ref.at[pl.ds(p * half * ROW_TILES, half * ROW_TILES)]

    def ffn(p):
        xb = _from_row_tiled(part(xs_ref, p), half).astype(BF16)
        h = jnp.dot(xb, w1b_ref[...], preferred_element_type=F32) + b1_ref[0]
        glu = jnp.minimum(h[:, :D_FF], SWIGLU_LIMIT)
        lin = jnp.clip(h[:, D_FF:], -SWIGLU_LIMIT, SWIGLU_LIMIT)
        act = glu / (1.0 + jnp.exp(-SWIGLU_ALPHA * glu)) * (lin + 1.0)
        _to_row_tiled(part(y_ref, p), jnp.dot(act.astype(BF16), w2b_ref[...], preferred_element_type=F32)
                      + b2_ref[0])

    def clear(p):
        part(y_ref, p)[...] = jnp.zeros((half * ROW_TILES, LANES), F32)

    pl.when(used)(lambda: ffn(0))
    pl.when(rows > half)(lambda: ffn(1))
    pl.when(jnp.logical_not(used))(lambda: clear(0))
    pl.when(rows <= half)(lambda: clear(1))


def _experts(layer, blk_tab, xs, w1_all, b1_all, w2_all, b2_all):
    n_slots = xs.shape[0] // ROW_TILES
    bs = SLOT_BLOCK
    d = D_MODEL
    w_elems = d * 2 * D_FF + D_FF * d
    vmem = _vmem_limit(2 * bs * d * 4, 2 * w_elems * 4, w_elems * 2, 2 * bs * d * 4, 2 * bs * 2 * D_FF * 4)
    bmap = lambda i, blk: (layer, blk[BLK_EXPERT, i], 0, 0)
    slot_blk = pl.BlockSpec((bs * ROW_TILES, LANES), lambda i, blk: (i, 0))
    any_spec = pl.BlockSpec(memory_space=pl.ANY)
    grid_spec = pltpu.PrefetchScalarGridSpec(
        num_scalar_prefetch=1,
        grid=(n_slots // bs,),
        in_specs=[slot_blk,
                  any_spec,
                  pl.BlockSpec((None, 1, 1, 2 * D_FF), bmap),
                  any_spec,
                  pl.BlockSpec((None, 1, 1, d), bmap)],
        out_specs=slot_blk,
        scratch_shapes=[pltpu.VMEM((2, d, 2 * D_FF), F32), pltpu.VMEM((2, D_FF, d), F32),
                        pltpu.VMEM((d, 2 * D_FF), BF16), pltpu.VMEM((D_FF, d), BF16),
                        pltpu.SemaphoreType.DMA((2, 2))],
    )
    return pl.pallas_call(
        functools.partial(_expert_kernel, layer=layer),
        grid_spec=grid_spec,
        out_shape=jax.ShapeDtypeStruct((n_slots * ROW_TILES, LANES), F32),
        compiler_params=_params(vmem, 1),
    )(blk_tab, xs, w1_all, b1_all, w2_all, b2_all)


def _combine_kernel(tab_ref, y_hbm, pos_ref, gates_ref, x_ref, g_ref, beta_ref, o_ref, ybuf, sems, *, nblk):
    i = pl.program_id(0)
    par = i % 2
    tb = x_ref.shape[0]

    def fetch(blk, slot):
        dst = ybuf.at[slot]
        _segment_loop(tab_ref, blk, lambda n, off, first: _run_copies(n, y_hbm, first, dst, off, sems.at[slot],
                                                                      SEG_BITS, _start))

    @pl.when(i == 0)
    def _():
        fetch(0, 0)

    @pl.when(i + 1 < nblk)
    def _():
        fetch(i + 1, 1 - par)

    pltpu.make_async_copy(_rows(y_hbm, 0, SORT_ROWS), ybuf.at[par], sems.at[par]).wait()
    ysorted = _from_row_tiled(ybuf.at[par], SORT_ROWS)

    pos = pos_ref[...]
    gates = gates_ref[...]
    pj = lax.broadcasted_iota(I32, (tb, SORT_ROWS), 1)
    wsel = jnp.where(pj == pos[:, 0:1], gates[:, 0:1], 0.0)
    for k in range(1, TOP_K):
        wsel = wsel + jnp.where(pj == pos[:, k:k + 1], gates[:, k:k + 1], 0.0)
    f = jnp.dot(wsel.astype(BF16), ysorted.astype(BF16), preferred_element_type=F32)
    o_ref[...] = _layer_norm_rows(DEEPNORM_ALPHA * x_ref[...] + f, g_ref[...], beta_ref[...])


def _combine_ln(tab, y, pos_tok, gates_tok, x, g, beta):
    t, d = x.shape
    tb = TOK_BLOCK
    nblk = t // tb
    vec = pl.BlockSpec((1, d), lambda i, tab: (0, 0))
    tok4 = pl.BlockSpec((tb, TOP_K), lambda i, tab: (i, 0))
    vmem = _vmem_limit(2 * SORT_ROWS * d * 4, 4 * tb * d * 4, 3 * SORT_ROWS * d * 4, 2 * tb * SORT_ROWS * 4)
    grid_spec = pltpu.PrefetchScalarGridSpec(
        num_scalar_prefetch=1,
        grid=(nblk,),
        in_specs=[pl.BlockSpec(memory_space=pl.ANY), tok4, tok4,
                  pl.BlockSpec((tb, d), lambda i, tab: (i, 0)),
                  vec, vec],
        out_specs=pl.BlockSpec((tb, d), lambda i, tab: (i, 0)),
        scratch_shapes=[pltpu.VMEM((2, SORT_ROWS * ROW_TILES, LANES), F32),
                        pltpu.SemaphoreType.DMA((2,))],
    )
    return pl.pallas_call(
        functools.partial(_combine_kernel, nblk=nblk),
        grid_spec=grid_spec,
        out_shape=jax.ShapeDtypeStruct((t, d), F32),
        compiler_params=_params(vmem, 1),
    )(tab, y, pos_tok, gates_tok, x, g, beta)


def _moe_ln(layer, x, w_router, b_router, w1_all, b1_all, w2_all, b2_all, g, beta):
    t, d = x.shape
    bs = SLOT_BLOCK
    n_blocks = (t * TOP_K) // bs + N_EXPERTS
    n_slots = n_blocks * bs

    pos, gates, nbe, off, base, cnt = _router(x, w_router.T, b_router.reshape(N_EXPERTS, 1))
    blk_tab, exp_tab = _plan(cnt[:, 0].astype(I32), n_blocks)

    run_slot = exp_tab[EXP_START, :N_EXPERTS][None, :] + base[:, :, 0].astype(I32)
    tab = jnp.concatenate([nbe[:, :, 0].astype(I32), off[:, :, 0].astype(I32), run_slot], axis=1).reshape(-1)

    xs = _dispatch(tab, exp_tab, pos, x, n_slots)
    y = _experts(layer, blk_tab, xs, w1_all, b1_all, w2_all, b2_all)
    pos_tok = pos.transpose(0, 2, 1).reshape(t, TOP_K)
    gates_tok = gates.transpose(0, 2, 1).reshape(t, TOP_K)
    return _combine_ln(tab, y, pos_tok, gates_tok, x, g, beta)


def kernel(x, w_ret_in, w_ret_out, w_swa_in, b_swa_in, w_swa_out, b_swa_out, swa_sinks, ln_mix_g, ln_mix_b,
           w_router, b_router, w_exp_in, b_exp_in, w_exp_out, b_exp_out, ln_ffn_g, ln_ffn_b):
    batch, seq, d = x.shape
    t = batch * seq
    xt = x.reshape(t, d)
    row = lambda v: v.reshape(1, -1)
    zeros_d = jnp.zeros((1, d), F32)
    w1_all, w2_all = w_exp_in, w_exp_out
    b1_all = b_exp_in.reshape(DEPTH, N_EXPERTS, 1, 2 * D_FF)
    b2_all = b_exp_out.reshape(DEPTH, N_EXPERTS, 1, d)
    for i in range(DEPTH):
        j = i // 2
        if i % 2 == 0:
            proj = _proj(xt, w_ret_in[j].astype(BF16), jnp.zeros((1, RET_IN), F32), PROJ_N_CHUNK)
            mixed = _retention_core(proj, batch, seq)
            w_out, b_out = w_ret_out[j].astype(BF16), zeros_d
        else:
            proj = _proj(xt, w_swa_in[j].astype(BF16), row(b_swa_in[j]), SWA_IN)
            mixed = _swa_core(proj, swa_sinks[j], batch, seq)
            w_out, b_out = w_swa_out[j].astype(BF16), row(b_swa_out[j])
        xt = _outproj_ln(mixed, w_out, b_out, xt, row(ln_mix_g[i]), row(ln_mix_b[i]))
        xt = _moe_ln(i, xt, w_router[i], b_router[i], w1_all, b1_all, w2_all, b2_all,
                     row(ln_ffn_g[i]), row(ln_ffn_b[i]))
    return xt.reshape(batch, seq, d)
```

```python
import functools
import math

import jax
import jax.numpy as jnp
from jax import lax
from jax.experimental import pallas as pl
from jax.experimental.pallas import tpu as pltpu

F32 = jnp.float32
BF16 = jnp.bfloat16
I32 = jnp.int32
U32 = jnp.uint32

D_MODEL = 1024
DEPTH = 2
RET_HEADS = 4
RET_QK_DIM = 256
RET_V_DIM = 512
RET_CHUNK = 128
RET_QK_ALL = RET_HEADS * RET_QK_DIM
RET_V_ALL = RET_HEADS * RET_V_DIM
RET_IN = 2 * RET_QK_ALL + 2 * RET_V_ALL
SWA_Q_HEADS = 16
SWA_KV_HEADS = 2
SWA_GROUP = 8
SWA_HEAD_DIM = 64
SWA_BLOCK = 128
WINDOW = 128
SWA_Q_ALL = SWA_Q_HEADS * SWA_HEAD_DIM
SWA_KV_ALL = 2 * SWA_KV_HEADS * SWA_HEAD_DIM
SWA_IN = SWA_Q_ALL + SWA_KV_ALL
N_EXPERTS = 32
TOP_K = 4
D_FF = 1024
SWIGLU_ALPHA = 1.702
SWIGLU_LIMIT = 7.0
LN_EPS = 1e-5
GN_EPS = 1e-6
DEEPNORM_ALPHA = (2 * DEPTH) ** 0.25

V7X_VMEM_BYTES = 64 * 1024 * 1024
VMEM_CAP = V7X_VMEM_BYTES - 8 * 1024 * 1024
ROW_TILE = 512
RET_KERNEL_CHUNK = 256
PROJ_N_CHUNK = 1536
TOK_BLOCK = 256
ROUTER_BLOCKS = 4
SLOT_BLOCK = 512
SUBLANES = 8
LANES = 128
ROW_TILES = D_MODEL // LANES
assert ROW_TILES == SUBLANES
SORT_ROWS = TOP_K * TOK_BLOCK
SEG_BITS = tuple(1 << b for b in range(TOK_BLOCK.bit_length() - 1, -1, -1))
FILL_BITS = tuple(1 << b for b in range(SLOT_BLOCK.bit_length() - 2, -1, -1))
SEG_TAB = 3 * N_EXPERTS

NT_DIMS = (((1,), (1,)), ((), ()))
TN_DIMS = (((0,), (0,)), ((), ()))


def _vmem_limit(*nbytes):
    est = int(sum(nbytes) * 1.2) + (4 << 20)
    return min(max(est, 16 << 20), VMEM_CAP)


def _params(vmem, n_grid):
    return pltpu.CompilerParams(dimension_semantics=("arbitrary",) * n_grid,
                                vmem_limit_bytes=vmem)


def _layer_norm_rows(y, g, b):
    mu = jnp.mean(y, axis=-1, keepdims=True)
    yc = y - mu
    var = jnp.mean(yc * yc, axis=-1, keepdims=True)
    return yc * lax.rsqrt(var + LN_EPS) * g + b


def _proj_kernel(x_ref, w_ref, b_ref, o_ref, *, n_chunk):
    xb = x_ref[...].astype(BF16)
    n_out = o_ref.shape[1]
    for c in range(0, n_out, n_chunk):
        acc = jnp.dot(xb, w_ref[:, c:c + n_chunk], preferred_element_type=F32)
        o_ref[:, c:c + n_chunk] = (acc + b_ref[:, c:c + n_chunk]).astype(o_ref.dtype)


def _proj(x, w_bf16, bias, n_chunk):
    t, d = x.shape
    n = w_bf16.shape[1]
    tm = ROW_TILE
    vmem = _vmem_limit(2 * tm * d * 4, 2 * d * n * 2, 2 * tm * n * 2, tm * n_chunk * 4 * 2, tm * d * 2)
    return pl.pallas_call(
        functools.partial(_proj_kernel, n_chunk=n_chunk),
        grid=(t // tm,),
        in_specs=[pl.BlockSpec((tm, d), lambda i: (i, 0)),
                  pl.BlockSpec((d, n), lambda i: (0, 0)),
                  pl.BlockSpec((1, n), lambda i: (0, 0))],
        out_specs=pl.BlockSpec((tm, n), lambda i: (i, 0)),
        out_shape=jax.ShapeDtypeStruct((t, n), BF16),
        compiler_params=_params(vmem, 1),
    )(x, w_bf16, bias)


def _ret_gammas():
    return [1.0 - 2.0 ** (-5.0 - h) for h in range(RET_HEADS)]


def _ret_kernel(q_ref, k_ref, v_ref, g_ref, dm_ref, qd_ref, kd_ref, o_ref, state_ref):
    @pl.when(pl.program_id(1) == 0)
    def _():
        state_ref[...] = jnp.zeros_like(state_ref)

    gammas = _ret_gammas()
    heads = range(RET_HEADS)
    qs = [q_ref[:, h * RET_QK_DIM:(h + 1) * RET_QK_DIM] for h in heads]
    ks = [k_ref[:, h * RET_QK_DIM:(h + 1) * RET_QK_DIM] for h in heads]
    vs = [v_ref[:, h * RET_V_DIM:(h + 1) * RET_V_DIM] for h in heads]
    scores = [(lax.dot_general(qs[h], ks[h], NT_DIMS, preferred_element_type=F32) * dm_ref[h]).astype(BF16)
              for h in heads]
    inter = [jnp.dot(qs[h], state_ref[h].astype(BF16), preferred_element_type=F32) * qd_ref[h] for h in heads]
    outs = [jnp.dot(scores[h], vs[h], preferred_element_type=F32) + inter[h] for h in heads]
    for h in heads:
        kdec = (ks[h].astype(F32) * kd_ref[h]).astype(BF16)
        upd = lax.dot_general(kdec, vs[h], TN_DIMS, preferred_element_type=F32)
        state_ref[h] = state_ref[h] * (gammas[h] ** q_ref.shape[0]) + upd
    for h in heads:
        o = outs[h]
        mu = jnp.mean(o, axis=-1, keepdims=True)
        oc = o - mu
        var = jnp.mean(oc * oc, axis=-1, keepdims=True)
        on = oc * lax.rsqrt(var + GN_EPS)
        gh = g_ref[:, h * RET_V_DIM:(h + 1) * RET_V_DIM].astype(F32)
        gate = gh / (1.0 + jnp.exp(-gh))
        o_ref[:, h * RET_V_DIM:(h + 1) * RET_V_DIM] = (gate * on).astype(o_ref.dtype)


def _retention_core(proj, batch, seq):
    c = RET_KERNEL_CHUNK
    nc = seq // c
    log_g = jnp.log(jnp.asarray(_ret_gammas(), F32))
    pos = jnp.arange(c, dtype=F32)
    rel = pos[:, None] - pos[None, :]
    scale = RET_QK_DIM ** -0.5
    dm = jnp.where(rel >= 0, jnp.exp(log_g[:, None, None] * jnp.maximum(rel, 0.0)), 0.0) * scale
    qd = jnp.exp(log_g[:, None] * (pos + 1.0))[..., None]
    kd = jnp.exp(log_g[:, None] * (c - 1.0 - pos))[..., None] * scale
    t = batch * seq
    row = lambda b, i: b * nc + i
    const3 = lambda b, i: (0, 0, 0)
    vmem = _vmem_limit(2 * c * RET_IN * 2, 2 * c * RET_V_ALL * 2, RET_HEADS * RET_QK_DIM * RET_V_DIM * 4 * 2,
                       8 << 20)
    return pl.pallas_call(
        _ret_kernel,
        grid=(batch, nc),
        in_specs=[pl.BlockSpec((c, RET_QK_ALL), lambda b, i: (row(b, i), 0)),
                  pl.BlockSpec((c, RET_QK_ALL), lambda b, i: (row(b, i), 1)),
                  pl.BlockSpec((c, RET_V_ALL), lambda b, i: (row(b, i), 1)),
                  pl.BlockSpec((c, RET_V_ALL), lambda b, i: (row(b, i), 2)),
                  pl.BlockSpec((RET_HEADS, c, c), const3),
                  pl.BlockSpec((RET_HEADS, c, 1), const3),
                  pl.BlockSpec((RET_HEADS, c, 1), const3)],
        out_specs=pl.BlockSpec((c, RET_V_ALL), lambda b, i: (row(b, i), 0)),
        out_shape=jax.ShapeDtypeStruct((t, RET_V_ALL), BF16),
        scratch_shapes=[pltpu.VMEM((RET_HEADS, RET_QK_DIM, RET_V_DIM), F32)],
        compiler_params=_params(vmem, 2),
    )(proj, proj, proj, proj, dm, qd, kd)


def _swa_kernel(sink_ref, bias_ref, q_ref, kvp_ref, kvc_ref, o_ref):
    c = SWA_BLOCK
    hd = SWA_HEAD_DIM
    qi = lax.broadcasted_iota(I32, (c, c), 0)
    kj = lax.broadcasted_iota(I32, (c, c), 1)
    cur = kj <= qi
    first_head = lax.broadcasted_iota(I32, (c, 2 * hd), 1) < hd
    ones = jnp.ones((2 * c, hd), BF16)
    zeros = jnp.zeros((2 * c, hd), BF16)
    scores, v_pairs = [], []
    for j in range(SWA_KV_HEADS):
        kcat = jnp.concatenate([kvp_ref[:, j * hd:(j + 1) * hd], kvc_ref[:, j * hd:(j + 1) * hd]], axis=0)
        k_t = kcat.astype(F32).T.astype(BF16)
        v0 = SWA_KV_HEADS * hd + j * hd
        vcat = jnp.concatenate([kvp_ref[:, v0:v0 + hd], kvc_ref[:, v0:v0 + hd]], axis=0)
        v_pairs.append(jnp.concatenate([jnp.concatenate([vcat, zeros, ones, zeros], axis=1),
                                        jnp.concatenate([zeros, vcat, zeros, ones], axis=1)], axis=0))
        for g in range(SWA_GROUP):
            h = j * SWA_GROUP + g
            qh = q_ref[:, h * hd:(h + 1) * hd] * (hd ** -0.5)
            s2 = jnp.dot(qh, k_t, preferred_element_type=F32)
            scores.append(jnp.where(cur, s2[:, c:], s2[:, :c]) + bias_ref[h])

    p2s, tails = [], []
    for h in range(SWA_Q_HEADS):
        sink = sink_ref[h]
        m = jnp.maximum(jnp.max(scores[h], axis=-1, keepdims=True), sink)
        p = jnp.exp(scores[h] - m)
        tails.append(jnp.exp(sink - m))
        p2s.append(jnp.concatenate([jnp.where(cur, 0.0, p), jnp.where(cur, p, 0.0)], axis=1).astype(BF16))

    outs = []
    for h in range(0, SWA_Q_HEADS, 2):
        ov = jnp.dot(jnp.concatenate(p2s[h:h + 2], axis=1), v_pairs[h // SWA_GROUP],
                     preferred_element_type=F32)
        den = ov[:, 2 * hd:] + jnp.where(first_head, tails[h], tails[h + 1])
        outs.append(ov[:, :2 * hd] / den)
    o_ref[...] = jnp.concatenate(outs, axis=-1).astype(o_ref.dtype)


def _swa_core(proj, sinks, batch, seq):
    c = SWA_BLOCK
    nb = seq // c
    t = batch * seq
    kv_col = SWA_Q_ALL // SWA_KV_ALL
    assert WINDOW == c
    qi = jnp.arange(c)[:, None]
    kj = jnp.arange(c)[None, :]
    cur = kj <= qi
    dist = jnp.where(cur, qi - kj, qi - kj + c).astype(F32)
    visible = jnp.stack([cur, jnp.ones_like(cur)])
    slopes = 2.0 ** (-8.0 * jnp.arange(1, SWA_Q_HEADS + 1, dtype=F32) / SWA_Q_HEADS)
    bias = jnp.where(visible[:, None], -slopes[None, :, None, None] * dist[None, None], -jnp.inf)
    vmem = _vmem_limit(2 * c * SWA_IN * 2 * 2, 2 * SWA_Q_HEADS * c * c * 4, 16 << 20)
    return pl.pallas_call(
        _swa_kernel,
        grid=(batch, nb),
        in_specs=[pl.BlockSpec(memory_space=pltpu.SMEM),
                  pl.BlockSpec((None, SWA_Q_HEADS, c, c), lambda b, n: (jnp.minimum(n, 1), 0, 0, 0)),
                  pl.BlockSpec((c, SWA_Q_ALL), lambda b, n: (b * nb + n, 0)),
                  pl.BlockSpec((c, SWA_KV_ALL), lambda b, n: (b * nb + jnp.maximum(n - 1, 0), kv_col)),
                  pl.BlockSpec((c, SWA_KV_ALL), lambda b, n: (b * nb + n, kv_col))],
        out_specs=pl.BlockSpec((c, SWA_Q_ALL), lambda b, n: (b * nb + n, 0)),
        out_shape=jax.ShapeDtypeStruct((t, SWA_Q_ALL), BF16),
        compiler_params=_params(vmem, 2),
    )(sinks, bias.astype(F32), proj, proj, proj)


def _outproj_ln_kernel(a_ref, w_ref, b_ref, x_ref, g_ref, beta_ref, o_ref):
    half = a_ref.shape[0] // 2
    for r in (0, half):
        m = jnp.dot(a_ref[r:r + half, :], w_ref[...], preferred_element_type=F32) + b_ref[...]
        o_ref[r:r + half, :] = _layer_norm_rows(DEEPNORM_ALPHA * x_ref[r:r + half, :] + m, g_ref[...],
                                                beta_ref[...])


def _outproj_ln(a, w_bf16, bias, x, g, beta):
    t, kin = a.shape
    d = D_MODEL
    tm = ROW_TILE
    vec = pl.BlockSpec((1, d), lambda i: (0, 0))
    vmem = _vmem_limit(2 * tm * kin * 2, 2 * kin * d * 2, 4 * tm * d * 4, 4 * tm * d * 4)
    return pl.pallas_call(
        _outproj_ln_kernel,
        grid=(t // tm,),
        in_specs=[pl.BlockSpec((tm, kin), lambda i: (i, 0)),
                  pl.BlockSpec((kin, d), lambda i: (0, 0)),
                  vec,
                  pl.BlockSpec((tm, d), lambda i: (i, 0)),
                  vec, vec],
        out_specs=pl.BlockSpec((tm, d), lambda i: (i, 0)),
        out_shape=jax.ShapeDtypeStruct((t, d), F32),
        compiler_params=_params(vmem, 1),
    )(a, w_bf16, bias, x, g, beta)


def _router_kernel(x_ref, wt_ref, b_ref, pos_ref, gate_ref, nbe_ref, off_ref, base_ref, cnt_ref, carry_ref):
    @pl.when(pl.program_id(0) == 0)
    def _():
        carry_ref[...] = jnp.zeros_like(carry_ref)

    tb = TOK_BLOCK
    wt = wt_ref[...]
    wh = wt.astype(BF16)
    wl = (wt - wh.astype(F32)).astype(BF16)
    nt = x_ref.shape[0]
    x = x_ref[...]
    xh = x.astype(BF16)
    xl = (x - xh.astype(F32)).astype(BF16)
    logits = (lax.dot_general(wh, xh, NT_DIMS, preferred_element_type=F32)
              + lax.dot_general(wh, xl, NT_DIMS, preferred_element_type=F32)
              + lax.dot_general(wl, xh, NT_DIMS, preferred_element_type=F32)
              + b_ref[...])
    eidx = lax.broadcasted_iota(I32, (N_EXPERTS, nt), 0).astype(F32)
    work = logits
    sels, vals = [], []
    for _ in range(TOP_K):
        m = jnp.max(work, axis=0, keepdims=True)
        idx = jnp.min(jnp.where(work == m, eidx, float(N_EXPERTS)), axis=0, keepdims=True)
        sel = eidx == idx
        sels.append(sel)
        vals.append(m)
        work = jnp.where(sel, -jnp.inf, work)

    exps = [jnp.exp(v - vals[0]) for v in vals]
    den = exps[0] + exps[1] + exps[2] + exps[3]
    gates = jnp.concatenate([e / den for e in exps], axis=0)

    mask = jnp.zeros((N_EXPERTS, nt), F32)
    for sel in sels:
        mask = mask + jnp.where(sel, 1.0, 0.0)

    ti = lax.broadcasted_iota(I32, (tb, tb), 0)
    tj = lax.broadcasted_iota(I32, (tb, tb), 1)
    upper = jnp.where(ti < tj, 1.0, 0.0).astype(BF16)
    ei = lax.broadcasted_iota(I32, (N_EXPERTS, N_EXPERTS), 0)
    ej = lax.broadcasted_iota(I32, (N_EXPERTS, N_EXPERTS), 1)
    lower = jnp.where(ej < ei, 1.0, 0.0).astype(BF16)
    carry = carry_ref[...]

    for u in range(ROUTER_BLOCKS):
        blk = slice(u * tb, (u + 1) * tb)
        mask_u = mask[:, blk]
        rank_loc = jnp.dot(mask_u.astype(BF16), upper, preferred_element_type=F32)
        n_be = jnp.broadcast_to(jnp.sum(mask_u, axis=1, keepdims=True), (N_EXPERTS, LANES))
        off_be = jnp.dot(lower, n_be.astype(BF16), preferred_element_type=F32)
        pos_all = rank_loc + off_be[:, :1]
        poss = [jnp.sum(jnp.where(sel[:, blk], pos_all, 0.0), axis=0, keepdims=True) for sel in sels]

        pos_ref[u] = jnp.concatenate(poss, axis=0).astype(I32)
        gate_ref[u] = gates[:, blk]
        nbe_ref[u] = n_be
        off_ref[u] = off_be
        base_ref[u] = carry
        carry = carry + n_be

    carry_ref[...] = carry
    cnt_ref[...] = carry


def _router(x, w_router_t, b_router_col):
    t, d = x.shape
    tb = TOK_BLOCK
    rb = ROUTER_BLOCKS
    assert tb <= 256, "per-block expert counts must stay exact in bf16"
    nblk = t // tb
    blk3 = pl.BlockSpec((rb, TOP_K, tb), lambda i: (i, 0, 0))
    seg3 = pl.BlockSpec((rb, N_EXPERTS, LANES), lambda i: (i, 0, 0))
    vmem = _vmem_limit(2 * rb * tb * d * 4, 2 * N_EXPERTS * d * 4, 8 << 20)
    return pl.pallas_call(
        _router_kernel,
        grid=(nblk // rb,),
        in_specs=[pl.BlockSpec((rb * tb, d), lambda i: (i, 0)),
                  pl.BlockSpec((N_EXPERTS, d), lambda i: (0, 0)),
                  pl.BlockSpec((N_EXPERTS, 1), lambda i: (0, 0))],
        out_specs=[blk3, blk3, seg3, seg3, seg3, pl.BlockSpec((N_EXPERTS, LANES), lambda i: (0, 0))],
        out_shape=[jax.ShapeDtypeStruct((nblk, TOP_K, tb), I32),
                   jax.ShapeDtypeStruct((nblk, TOP_K, tb), F32),
                   jax.ShapeDtypeStruct((nblk, N_EXPERTS, LANES), F32),
                   jax.ShapeDtypeStruct((nblk, N_EXPERTS, LANES), F32),
                   jax.ShapeDtypeStruct((nblk, N_EXPERTS, LANES), F32),
                   jax.ShapeDtypeStruct((N_EXPERTS, LANES), F32)],
        scratch_shapes=[pltpu.VMEM((N_EXPERTS, LANES), F32)],
        compiler_params=_params(vmem, 1),
    )(x, w_router_t, b_router_col)


BLK_EXPERT, BLK_ROWS, BLK_NEXT, BLK_SLOT = range(4)
EXP_START, EXP_FIRST_PAD, EXP_NUM_PAD, EXP_SLOT = range(4)


def _plan_kernel(cnt_ref, blk_ref, exp_ref, *, n_blocks):
    bs = SLOT_BLOCK
    shift = bs.bit_length() - 1

    def init(i, carry):
        blk_ref[BLK_EXPERT, i] = N_EXPERTS - 1
        blk_ref[BLK_ROWS, i] = 0
        blk_ref[BLK_NEXT, i] = -1
        blk_ref[BLK_SLOT, i] = 0
        return carry

    lax.fori_loop(0, n_blocks, init, 0)

    def forward(e, carry):
        start, n_used = carry
        n = cnt_ref[e]
        padded = ((n + (bs - 1)) >> shift) << shift
        exp_ref[EXP_START, e] = start
        exp_ref[EXP_FIRST_PAD, e] = start + n
        exp_ref[EXP_NUM_PAD, e] = padded - n
        exp_ref[EXP_SLOT, e] = n_used & 1
        return start + padded, n_used + (n > 0).astype(I32)

    total, _ = lax.fori_loop(0, N_EXPERTS, forward, (jnp.int32(0), jnp.int32(0)))
    exp_ref[EXP_START, N_EXPERTS] = total
    exp_ref[EXP_FIRST_PAD, N_EXPERTS] = total
    exp_ref[EXP_NUM_PAD, N_EXPERTS] = 0
    exp_ref[EXP_SLOT, N_EXPERTS] = 0

    def backward(k, nxt):
        e = N_EXPERTS - 1 - k
        n = cnt_ref[e]
        first_blk = exp_ref[EXP_START, e] >> shift
        slot = exp_ref[EXP_SLOT, e]

        def per_block(j, carry):
            i = first_blk + j
            blk_ref[BLK_EXPERT, i] = e
            blk_ref[BLK_ROWS, i] = jnp.minimum(n - (j << shift), bs)
            blk_ref[BLK_NEXT, i] = nxt
            blk_ref[BLK_SLOT, i] = slot
            return carry

        lax.fori_loop(0, (n + (bs - 1)) >> shift, per_block, 0)
        return jnp.where(n > 0, e, nxt)

    lax.fori_loop(0, N_EXPERTS, backward, jnp.int32(-1))


def _plan(counts, n_blocks):
    assert SLOT_BLOCK & (SLOT_BLOCK - 1) == 0
    smem = pl.BlockSpec(memory_space=pltpu.SMEM)
    return pl.pallas_call(
        functools.partial(_plan_kernel, n_blocks=n_blocks),
        in_specs=[smem],
        out_specs=[smem, smem],
        out_shape=[jax.ShapeDtypeStruct((4, n_blocks), I32),
                   jax.ShapeDtypeStruct((4, N_EXPERTS + 1), I32)],
    )(counts)


def _rows(ref, row, n):
    return ref.at[pl.ds(pl.multiple_of(row * ROW_TILES, ROW_TILES), n * ROW_TILES)]


def _to_row_tiled(ref, value):
    n = value.shape[0]
    for c in range(ROW_TILES):
        ref[pl.ds(c, n, stride=ROW_TILES), :] = value[:, c * LANES:(c + 1) * LANES]


def _from_row_tiled(ref, n):
    return jnp.concatenate([ref[pl.ds(c, n, stride=ROW_TILES), :] for c in range(ROW_TILES)], axis=1)


def _run_copies(n, src_ref, src_row, dst_ref, dst_row, sem, bits, act, advance_src=True):
    for bit in bits:
        @pl.when((n & bit) != 0)
        def _():
            act(pltpu.make_async_copy(_rows(src_ref, src_row, bit), _rows(dst_ref, dst_row, bit), sem))
        if advance_src:
            src_row = src_row + (n & bit)
        dst_row = dst_row + (n & bit)


def _start(cp):
    cp.start()


def _wait(cp):
    cp.wait()


def _segment_loop(tab_ref, blk, act_on_run):
    base = blk * SEG_TAB

    def body(e, carry):
        act_on_run(tab_ref[base + e], tab_ref[base + N_EXPERTS + e], tab_ref[base + 2 * N_EXPERTS + e])
        return carry

    lax.fori_loop(0, N_EXPERTS, body, 0)


def _dispatch_kernel(tab_ref, fill_ref, pos_ref, x_ref, xs_hbm, sbuf, zeros_ref, sems, fill_sem, *, nblk):
    i = pl.program_id(0)
    par = i % 2
    tb = x_ref.shape[0]
    zrows = zeros_ref.shape[0] // ROW_TILES

    @pl.when(i == 0)
    def _():
        zeros_ref[...] = jnp.zeros_like(zeros_ref)

        def fill(act):
            def body(e, carry):
                _run_copies(fill_ref[EXP_NUM_PAD, e], zeros_ref, 0, xs_hbm, fill_ref[EXP_FIRST_PAD, e], fill_sem,
                            FILL_BITS, act, advance_src=False)
                return carry
            lax.fori_loop(0, N_EXPERTS, body, 0)

        fill(_start)
        fill(_wait)

        first_tail = fill_ref[EXP_START, N_EXPERTS] // zrows
        n_chunks = xs_hbm.shape[0] // (zrows * ROW_TILES)

        def tail(act):
            def body(j, carry):
                act(pltpu.make_async_copy(zeros_ref, _rows(xs_hbm, j * zrows, zrows), fill_sem))
                return carry
            lax.fori_loop(first_tail, n_chunks, body, 0)

        tail(_start)
        tail(_wait)

    def drain(slot):
        pltpu.make_async_copy(sbuf.at[slot], _rows(xs_hbm, 0, SORT_ROWS), sems.at[slot]).wait()

    @pl.when(i >= 2)
    def _():
        drain(par)

    pos = pos_ref[0]
    pi = lax.broadcasted_iota(I32, (SORT_ROWS, tb), 0)
    onehot = jnp.where(pi == pos[0:1, :], 1.0, 0.0)
    for k in range(1, TOP_K):
        onehot = onehot + jnp.where(pi == pos[k:k + 1, :], 1.0, 0.0)
    xsorted = jnp.dot(onehot.astype(BF16), x_ref[...].astype(BF16), preferred_element_type=F32)
    buf = sbuf.at[par]
    _to_row_tiled(buf, xsorted)

    _segment_loop(tab_ref, i, lambda n, off, slot: _run_copies(n, buf, off, xs_hbm, slot, sems.at[par],
                                                               SEG_BITS, _start))

    @pl.when(i == nblk - 1)
    def _():
        drain(par)
        if nblk >= 2:
            drain(1 - par)


def _dispatch(tab, fill, pos, x, n_slots):
    nblk, _, tb = pos.shape
    d = x.shape[1]
    zrows = SLOT_BLOCK // 2
    vmem = _vmem_limit(2 * tb * d * 4, 2 * SORT_ROWS * d * 4, zrows * d * 4, 3 * SORT_ROWS * d * 4)
    grid_spec = pltpu.PrefetchScalarGridSpec(
        num_scalar_prefetch=2,
        grid=(nblk,),
        in_specs=[pl.BlockSpec((1, TOP_K, tb), lambda i, tab, fill: (i, 0, 0)),
                  pl.BlockSpec((tb, d), lambda i, tab, fill: (i, 0))],
        out_specs=pl.BlockSpec(memory_space=pl.ANY),
        scratch_shapes=[pltpu.VMEM((2, SORT_ROWS * ROW_TILES, LANES), F32),
                        pltpu.VMEM((zrows * ROW_TILES, LANES), F32),
                        pltpu.SemaphoreType.DMA((2,)),
                        pltpu.SemaphoreType.DMA(())],
    )
    return pl.pallas_call(
        functools.partial(_dispatch_kernel, nblk=nblk),
        grid_spec=grid_spec,
        out_shape=jax.ShapeDtypeStruct((n_slots * ROW_TILES, LANES), F32),
        compiler_params=pltpu.CompilerParams(dimension_semantics=("arbitrary",), has_side_effects=True,
                                             vmem_limit_bytes=vmem),
    )(tab, fill, pos, x)


def _expert_kernel(blk_ref, xs_ref, w1_hbm, b1_ref, w2_hbm, b2_ref, y_ref,
                   w1f_ref, w2f_ref, w1b_ref, w2b_ref, sems, *, layer):
    i = pl.program_id(0)
    expert = blk_ref[BLK_EXPERT, i]
    rows = blk_ref[BLK_ROWS, i]
    nxt = blk_ref[BLK_NEXT, i]
    slot = blk_ref[BLK_SLOT, i]
    used = rows > 0
    half = SLOT_BLOCK // 2

    def weight_copies(e, s):
        return (pltpu.make_async_copy(w1_hbm.at[layer, e], w1f_ref.at[s], sems.at[0, s]),
                pltpu.make_async_copy(w2_hbm.at[layer, e], w2f_ref.at[s], sems.at[1, s]))

    @pl.when(used & ((i == 0) | (expert != blk_ref[BLK_EXPERT, jnp.maximum(i - 1, 0)])))
    def _():
        @pl.when(i == 0)
        def _():
            for cp in weight_copies(expert, slot):
                cp.start()

        @pl.when(nxt >= 0)
        def _():
            for cp in weight_copies(nxt, 1 - slot):
                cp.start()

        for cp in weight_copies(expert, slot):
            cp.wait()
        w1b_ref[...] = w1f_ref[slot].astype(BF16)
        w2b_ref[...] = w2f_ref[slot].astype(BF16)

    def part(ref, p):
        return ref.at[pl.ds(p * half * ROW_TILES, half * ROW_TILES)]

    def ffn(p):
        xb = _from_row_tiled(part(xs_ref, p), half).astype(BF16)
        h = jnp.dot(xb, w1b_ref[...], preferred_element_type=F32) + b1_ref[0]
        glu = jnp.minimum(h[:, :D_FF], SWIGLU_LIMIT)
        lin = jnp.clip(h[:, D_FF:], -SWIGLU_LIMIT, SWIGLU_LIMIT)
        act = glu / (1.0 + jnp.exp(-SWIGLU_ALPHA * glu)) * (lin + 1.0)
        _to_row_tiled(part(y_ref, p), jnp.dot(act.astype(BF16), w2b_ref[...], preferred_element_type=F32)
                      + b2_ref[0])

    def clear(p):
        part(y_ref, p)[...] = jnp.zeros((half * ROW_TILES, LANES), F32)

    pl.when(used)(lambda: ffn(0))
    pl.when(rows > half)(lambda: ffn(1))
    pl.when(jnp.logical_not(used))(lambda: clear(0))
    pl.when(rows <= half)(lambda: clear(1))


def _experts(layer, blk_tab, xs, w1_all, b1_all, w2_all, b2_all):
    n_slots = xs.shape[0] // ROW_TILES
    bs = SLOT_BLOCK
    d = D_MODEL
    w_elems = d * 2 * D_FF + D_FF * d
    vmem = _vmem_limit(2 * bs * d * 4, 2 * w_elems * 4, w_elems * 2, 2 * bs * d * 4, 2 * bs * 2 * D_FF * 4)
    bmap = lambda i, blk: (layer, blk[BLK_EXPERT, i], 0, 0)
    slot_blk = pl.BlockSpec((bs * ROW_TILES, LANES), lambda i, blk: (i, 0))
    any_spec = pl.BlockSpec(memory_space=pl.ANY)
    grid_spec = pltpu.PrefetchScalarGridSpec(
        num_scalar_prefetch=1,
        grid=(n_slots // bs,),
        in_specs=[slot_blk,
                  any_spec,
                  pl.BlockSpec((None, 1, 1, 2 * D_FF), bmap),
                  any_spec,
                  pl.BlockSpec((None, 1, 1, d), bmap)],
        out_specs=slot_blk,
        scratch_shapes=[pltpu.VMEM((2, d, 2 * D_FF), F32), pltpu.VMEM((2, D_FF, d), F32),
                        pltpu.VMEM((d, 2 * D_FF), BF16), pltpu.VMEM((D_FF, d), BF16),
                        pltpu.SemaphoreType.DMA((2, 2))],
    )
    return pl.pallas_call(
        functools.partial(_expert_kernel, layer=layer),
        grid_spec=grid_spec,
        out_shape=jax.ShapeDtypeStruct((n_slots * ROW_TILES, LANES), F32),
        compiler_params=_params(vmem, 1),
    )(blk_tab, xs, w1_all, b1_all, w2_all, b2_all)


def _combine_kernel(tab_ref, y_hbm, pos_ref, gates_ref, x_ref, g_ref, beta_ref, o_ref, ybuf, sems, *, nblk):
    i = pl.program_id(0)
    par = i % 2
    tb = x_ref.shape[0]

    def fetch(blk, slot):
        dst = ybuf.at[slot]
        _segment_loop(tab_ref, blk, lambda n, off, first: _run_copies(n, y_hbm, first, dst, off, sems.at[slot],
                                                                      SEG_BITS, _start))

    @pl.when(i == 0)
    def _():
        fetch(0, 0)

    @pl.when(i + 1 < nblk)
    def _():
        fetch(i + 1, 1 - par)

    pltpu.make_async_copy(_rows(y_hbm, 0, SORT_ROWS), ybuf.at[par], sems.at[par]).wait()
    ysorted = _from_row_tiled(ybuf.at[par], SORT_ROWS)

    pos = pos_ref[0]
    gates = gates_ref[0]
    pi = lax.broadcasted_iota(I32, (SORT_ROWS, tb), 0)
    wsel = jnp.where(pi == pos[0:1, :], gates[0:1, :], 0.0)
    for k in range(1, TOP_K):
        wsel = wsel + jnp.where(pi == pos[k:k + 1, :], gates[k:k + 1, :], 0.0)
    f = lax.dot_general(wsel.astype(BF16), ysorted.astype(BF16), TN_DIMS, preferred_element_type=F32)
    o_ref[...] = _layer_norm_rows(DEEPNORM_ALPHA * x_ref[...] + f, g_ref[...], beta_ref[...])


def _combine_ln(tab, y, pos, gates, x, g, beta):
    t, d = x.shape
    tb = TOK_BLOCK
    nblk = t // tb
    vec = pl.BlockSpec((1, d), lambda i, tab: (0, 0))
    tok4 = pl.BlockSpec((1, TOP_K, tb), lambda i, tab: (i, 0, 0))
    vmem = _vmem_limit(2 * SORT_ROWS * d * 4, 4 * tb * d * 4, 3 * SORT_ROWS * d * 4, 2 * tb * SORT_ROWS * 4)
    grid_spec = pltpu.PrefetchScalarGridSpec(
        num_scalar_prefetch=1,
        grid=(nblk,),
        in_specs=[pl.BlockSpec(memory_space=pl.ANY), tok4, tok4,
                  pl.BlockSpec((tb, d), lambda i, tab: (i, 0)),
                  vec, vec],
        out_specs=pl.BlockSpec((tb, d), lambda i, tab: (i, 0)),
        scratch_shapes=[pltpu.VMEM((2, SORT_ROWS * ROW_TILES, LANES), F32),
                        pltpu.SemaphoreType.DMA((2,))],
    )
    return pl.pallas_call(
        functools.partial(_combine_kernel, nblk=nblk),
        grid_spec=grid_spec,
        out_shape=jax.ShapeDtypeStruct((t, d), F32),
        compiler_params=_params(vmem, 1),
    )(tab, y, pos, gates, x, g, beta)


def _moe_ln(layer, x, w_router, b_router, w1_all, b1_all, w2_all, b2_all, g, beta):
    t, d = x.shape
    bs = SLOT_BLOCK
    n_blocks = (t * TOP_K) // bs + N_EXPERTS
    n_slots = n_blocks * bs

    pos, gates, nbe, off, base, cnt = _router(x, w_router.T, b_router.reshape(N_EXPERTS, 1))
    blk_tab, exp_tab = _plan(cnt[:, 0].astype(I32), n_blocks)

    run_slot = exp_tab[EXP_START, :N_EXPERTS][None, :] + base[:, :, 0].astype(I32)
    tab = jnp.concatenate([nbe[:, :, 0].astype(I32), off[:, :, 0].astype(I32), run_slot], axis=1).reshape(-1)

    xs = _dispatch(tab, exp_tab, pos, x, n_slots)
    y = _experts(layer, blk_tab, xs, w1_all, b1_all, w2_all, b2_all)
    return _combine_ln(tab, y, pos, gates, x, g, beta)


def kernel(x, w_ret_in, w_ret_out, w_swa_in, b_swa_in, w_swa_out, b_swa_out, swa_sinks, ln_mix_g, ln_mix_b,
           w_router, b_router, w_exp_in, b_exp_in, w_exp_out, b_exp_out, ln_ffn_g, ln_ffn_b):
    batch, seq, d = x.shape
    t = batch * seq
    xt = x.reshape(t, d)
    row = lambda v: v.reshape(1, -1)
    zeros_d = jnp.zeros((1, d), F32)
    w1_all, w2_all = w_exp_in, w_exp_out
    b1_all = b_exp_in.reshape(DEPTH, N_EXPERTS, 1, 2 * D_FF)
    b2_all = b_exp_out.reshape(DEPTH, N_EXPERTS, 1, d)
    for i in range(DEPTH):
        j = i // 2
        if i % 2 == 0:
            proj = _proj(xt, w_ret_in[j].astype(BF16), jnp.zeros((1, RET_IN), F32), PROJ_N_CHUNK)
            mixed = _retention_core(proj, batch, seq)
            w_out, b_out = w_ret_out[j].astype(BF16), zeros_d
        else:
            proj = _proj(xt, w_swa_in[j].astype(BF16), row(b_swa_in[j]), SWA_IN)
            mixed = _swa_core(proj, swa_sinks[j], batch, seq)
            w_out, b_out = w_swa_out[j].astype(BF16), row(b_swa_out[j])
        xt = _outproj_ln(mixed, w_out, b_out, xt, row(ln_mix_g[i]), row(ln_mix_b[i]))
        xt = _moe_ln(i, xt, w_router[i], b_router[i], w1_all, b1_all, w2_all, b2_all,
                     row(ln_ffn_g[i]), row(ln_ffn_b[i]))
    return xt.reshape(batch, seq, d)
```
